```python
import numpy as np
import jax, jax.numpy as jnp
from jax import lax

D_MODEL = 2048
BATCH = 4
SEQ = 8192
DEPTH = 1
DEC_BATCH = 1
DEC_SEQ = 8192
PAST_LEN = 128

N_META = 16
GRID_W = 64
WIN_ROWS = 8
WIN_COLS = 16
N_Q_HEADS = 32
N_KV_HEADS = 8
Q_PER_KV = N_Q_HEADS // N_KV_HEADS
HEAD_DIM = 64
ATTN_W = N_Q_HEADS * HEAD_DIM
KV_W = N_KV_HEADS * HEAD_DIM
SSD_INNER = 2 * D_MODEL
SSD_HEAD_DIM = 64
SSD_HEADS = SSD_INNER // SSD_HEAD_DIM
SSD_GROUPS = 8
SSD_HEADS_PER_GROUP = SSD_HEADS // SSD_GROUPS
SSD_STATE = 128
D_CONV = 4
CHUNK = 128
CONV_DIM = SSD_INNER + 2 * SSD_GROUPS * SSD_STATE
IN_W = ATTN_W + 2 * KV_W + SSD_INNER + CONV_DIM + 2 * SSD_HEADS
MIX_W = ATTN_W + SSD_INNER
D_FF = -(-(-(-8 * D_MODEL // 3)) // 256) * 256
EPS = 1e-6

kernel_name = "hymba_na2d_bissd_encoder"


def rms_norm(x, g):
    xf = x.astype(jnp.float32)
    y = xf * lax.rsqrt(jnp.mean(xf * xf, axis=-1, keepdims=True) + EPS)
    return (y * g.astype(jnp.float32)).astype(x.dtype)


def neighbourhood_attention(q, k, v, k_meta, v_meta, rpb):
    b, n = q.shape[:2]
    rows = n // GRID_W
    kh = min(WIN_ROWS, rows)
    qg = q.reshape(b, rows, GRID_W, N_KV_HEADS, Q_PER_KV, HEAD_DIM)
    kg = k.reshape(b, rows, GRID_W, N_KV_HEADS, HEAD_DIM)
    vg = v.reshape(b, rows, GRID_W, N_KV_HEADS, HEAD_DIM)
    cols = np.arange(GRID_W)
    col_start = np.clip(cols - WIN_COLS // 2, 0, GRID_W - WIN_COLS)
    col_idx = (col_start[:, None] + np.arange(WIN_COLS)[None, :]).astype(np.int32)
    dc = (col_idx - cols[:, None] + WIN_COLS - 1).astype(np.int32)
    rpb_g = rpb.astype(jnp.float32).reshape(N_KV_HEADS, Q_PER_KV, 2 * WIN_ROWS - 1, 2 * WIN_COLS - 1)
    scale = HEAD_DIM ** -0.5
    n_win = kh * WIN_COLS

    def row_block(r):
        rs = jnp.clip(r - kh // 2, 0, rows - kh)
        q_r = lax.dynamic_index_in_dim(qg, r, axis=1, keepdims=False)
        k_band = lax.dynamic_slice_in_dim(kg, rs, kh, axis=1)
        v_band = lax.dynamic_slice_in_dim(vg, rs, kh, axis=1)
        k_win = k_band[:, :, col_idx]
        v_win = v_band[:, :, col_idx]
        dr = rs + jnp.arange(kh) - r + WIN_ROWS - 1
        bias = rpb_g[:, :, dr[None, :, None], dc[:, None, :]]
        s_win = jnp.einsum('bckrd,bicjkd->bkrcij', q_r, k_win).astype(jnp.float32) * scale + bias
        s_meta = jnp.einsum('bckrd,bmkd->bkrcm', q_r, k_meta).astype(jnp.float32) * scale
        s = jnp.concatenate([s_win.reshape(b, N_KV_HEADS, Q_PER_KV, GRID_W, n_win), s_meta], axis=-1)
        p = jax.nn.softmax(s, axis=-1).astype(v.dtype)
        p_win = p[..., :n_win].reshape(b, N_KV_HEADS, Q_PER_KV, GRID_W, kh, WIN_COLS)
        p_meta = p[..., n_win:]
        return (jnp.einsum('bkrcij,bicjkd->bckrd', p_win, v_win)
                + jnp.einsum('bkrcm,bmkd->bckrd', p_meta, v_meta))

    out = lax.map(row_block, jnp.arange(rows))
    return jnp.moveaxis(out, 0, 1).reshape(b, n, ATTN_W)


def meta_attention(q_meta, k_meta, v_meta):
    b = q_meta.shape[0]
    s = jnp.einsum('bmkrd,bnkd->bkrmn', q_meta, k_meta).astype(jnp.float32) * HEAD_DIM ** -0.5
    p = jax.nn.softmax(s, axis=-1).astype(v_meta.dtype)
    return jnp.einsum('bkrmn,bnkd->bmkrd', p, v_meta).reshape(b, N_META, ATTN_W)


def ssd_scan(x, dt, a, bm, cm):
    b, t = x.shape[:2]
    nc = t // CHUNK
    g, r = SSD_GROUPS, SSD_HEADS_PER_GROUP

    def to_chunks(z):
        return jnp.moveaxis(z.reshape((b, nc, CHUNK) + z.shape[2:]), 1, 0)

    xs = to_chunks(x.astype(jnp.float32) * dt[..., None]).reshape(nc, b, CHUNK, g, r, SSD_HEAD_DIM)
    da = to_chunks(dt * a).reshape(nc, b, CHUNK, g, r)
    bs = to_chunks(bm.astype(jnp.float32))
    cs = to_chunks(cm.astype(jnp.float32))
    lower = np.tril(np.ones((CHUNK, CHUNK), dtype=bool))[None, :, :, None, None]

    def step(state, inp):
        xc, dac, bc, cc = inp
        acs = jnp.cumsum(dac, axis=1)
        seg = acs[:, :, None] - acs[:, None, :]
        lmat = jnp.exp(jnp.where(lower, seg, -jnp.inf))
        cb = jnp.einsum('blgn,bsgn->blsg', cc, bc)
        y = jnp.einsum('blsg,blsgr,bsgrp->blgrp', cb, lmat, xc)
        y = y + jnp.einsum('blgn,bgrpn->blgrp', cc, state) * jnp.exp(acs)[..., None]
        last = acs[:, -1]
        decay_in = jnp.exp(last[:, None] - acs)
        state = (state * jnp.exp(last)[..., None, None]
                 + jnp.einsum('bsgn,bsgr,bsgrp->bgrpn', bc, decay_in, xc))
        return state, y

    init = jnp.zeros((b, g, r, SSD_HEAD_DIM, SSD_STATE), jnp.float32)
    _, ys = lax.scan(step, init, (xs, da, bs, cs))
    return jnp.moveaxis(ys, 0, 1).reshape(b, t, SSD_HEADS, SSD_HEAD_DIM)


def pad_front(a, p):
    return jnp.pad(a, [(0, 0), (p, 0)] + [(0, 0)] * (a.ndim - 2))


def ssd_mixer(z, xbc, dt_raw, conv_w, conv_b, dt_bias_f, dt_bias_b, a_log_f, a_log_b, d_skip, norm_w):
    b, t = xbc.shape[:2]
    left = D_CONV // 2
    xbc = lax.conv_general_dilated(
        xbc, conv_w.reshape(D_CONV, 1, CONV_DIM).astype(xbc.dtype), window_strides=(1,),
        padding=[(left, D_CONV - 1 - left)], dimension_numbers=('NWC', 'WIO', 'NWC'),
        feature_group_count=CONV_DIM)
    xbc = jax.nn.silu(xbc + conv_b)
    xs = xbc[..., :SSD_INNER].reshape(b, t, SSD_HEADS, SSD_HEAD_DIM)
    bm = xbc[..., SSD_INNER:SSD_INNER + SSD_GROUPS * SSD_STATE].reshape(b, t, SSD_GROUPS, SSD_STATE)
    cm = xbc[..., SSD_INNER + SSD_GROUPS * SSD_STATE:].reshape(b, t, SSD_GROUPS, SSD_STATE)
    dtf = dt_raw.astype(jnp.float32)
    dt_f = jax.nn.softplus(dtf[..., :SSD_HEADS] + dt_bias_f.astype(jnp.float32))
    dt_b = jax.nn.softplus(dtf[..., SSD_HEADS:] + dt_bias_b.astype(jnp.float32))
    a_f = -jnp.exp(a_log_f.astype(jnp.float32))
    a_b = -jnp.exp(a_log_b.astype(jnp.float32))
    p = CHUNK - N_META
    xs_p, bm_p, cm_p = pad_front(xs, p), pad_front(bm, p), pad_front(cm, p)
    y_f = ssd_scan(xs_p, pad_front(dt_f, p), a_f, bm_p, cm_p)
    flip = lambda u: jnp.flip(u, axis=1)
    y_b = flip(ssd_scan(flip(xs_p), flip(pad_front(dt_b, p)), a_b, flip(bm_p), flip(cm_p)))
    y = (y_f + y_b)[:, p:] + xs.astype(jnp.float32) * d_skip.astype(jnp.float32)[:, None]
    y = y.reshape(b, t, SSD_INNER) * jax.nn.silu(z.astype(jnp.float32))
    y = y.reshape(b, t, SSD_GROUPS, SSD_INNER // SSD_GROUPS)
    y = y * lax.rsqrt(jnp.mean(y * y, axis=-1, keepdims=True) + EPS)
    return (y.reshape(b, t, SSD_INNER) * norm_w.astype(jnp.float32)).astype(z.dtype)


def encoder_layer(h, g_mix, w_in, q_norm, k_norm, rpb, conv_w, conv_b, dt_bias_f, dt_bias_b,
                  a_log_f, a_log_b, d_skip, ssd_norm, w_out, g_ffn, w_gate, w_up, w_down):
    b, t, _ = h.shape
    u = rms_norm(h, g_mix)
    proj = u @ w_in
    o1 = ATTN_W
    o2 = o1 + KV_W
    o3 = o2 + KV_W
    o4 = o3 + SSD_INNER
    o5 = o4 + CONV_DIM
    q, k, v, z, xbc, dt_raw = jnp.split(proj, [o1, o2, o3, o4, o5], axis=-1)
    q = rms_norm(q.reshape(b, t, N_KV_HEADS, Q_PER_KV, HEAD_DIM), q_norm)
    k = rms_norm(k.reshape(b, t, N_KV_HEADS, HEAD_DIM), k_norm)
    v = v.reshape(b, t, N_KV_HEADS, HEAD_DIM)
    k_meta, v_meta = k[:, :N_META], v[:, :N_META]
    attn_real = neighbourhood_attention(q[:, N_META:], k[:, N_META:], v[:, N_META:], k_meta, v_meta, rpb)
    attn_meta = meta_attention(q[:, :N_META], k_meta, v_meta)
    attn = jnp.concatenate([attn_meta, attn_real], axis=1).astype(h.dtype)
    ssd = ssd_mixer(z, xbc, dt_raw, conv_w, conv_b, dt_bias_f, dt_bias_b, a_log_f, a_log_b, d_skip, ssd_norm)
    h = h + jnp.concatenate([attn, ssd], axis=-1) @ w_out
    f = rms_norm(h, g_ffn)
    return h + (jax.nn.silu(f @ w_gate) * (f @ w_up)) @ w_down


def run_trunk(x, meta_tokens, g_mix, w_in, q_norm, k_norm, rpb, conv_w, conv_b, dt_bias_f, dt_bias_b,
              a_log_f, a_log_b, d_skip, ssd_norm, w_out, g_ffn, w_gate, w_up, w_down):
    b = x.shape[0]
    meta = jnp.broadcast_to(meta_tokens.astype(x.dtype)[None], (b, N_META, D_MODEL))
    h = jnp.concatenate([meta, x], axis=1)
    for l in range(DEPTH):
        h = encoder_layer(h, g_mix[l], w_in[l], q_norm[l], k_norm[l], rpb[l], conv_w[l], conv_b[l],
                          dt_bias_f[l], dt_bias_b[l], a_log_f[l], a_log_b[l], d_skip[l], ssd_norm[l],
                          w_out[l], g_ffn[l], w_gate[l], w_up[l], w_down[l])
    return h[:, N_META:]


def setup_inputs(seed: int = 0) -> dict:
    key = jax.random.key(seed)
    ks = jax.random.split(key, 24)
    f32 = jnp.float32
    nrm = lambda k, s, sc: jax.random.normal(k, s, f32) * sc
    dt0 = jnp.exp(jax.random.uniform(ks[10], (2, DEPTH, SSD_HEADS), f32)
                  * (jnp.log(0.1) - jnp.log(0.001)) + jnp.log(0.001))
    dt_bias = dt0 + jnp.log(-jnp.expm1(-dt0))
    a_log = jnp.log(jax.random.uniform(ks[11], (2, DEPTH, SSD_HEADS), f32, 1.0, 16.0))
    return {
        "x_prompt": nrm(ks[0], (BATCH, SEQ, D_MODEL), 1.0),
        "x_sample": nrm(ks[1], (DEC_BATCH, DEC_SEQ, D_MODEL), 1.0),
        "meta_tokens": nrm(ks[2], (N_META, D_MODEL), 1.0),
        "g_mix": 1.0 + nrm(ks[3], (DEPTH, D_MODEL), 0.02),
        "w_in": nrm(ks[4], (DEPTH, D_MODEL, IN_W), D_MODEL ** -0.5),
        "q_norm": 1.0 + nrm(ks[5], (DEPTH, HEAD_DIM), 0.02),
        "k_norm": 1.0 + nrm(ks[6], (DEPTH, HEAD_DIM), 0.02),
        "rpb": nrm(ks[7], (DEPTH, N_Q_HEADS, 2 * WIN_ROWS - 1, 2 * WIN_COLS - 1), 0.02),
        "conv_w": nrm(ks[8], (DEPTH, D_CONV, CONV_DIM), D_CONV ** -0.5),
        "conv_b": nrm(ks[9], (DEPTH, CONV_DIM), 0.02),
        "dt_bias_f": dt_bias[0],
        "dt_bias_b": dt_bias[1],
        "a_log_f": a_log[0],
        "a_log_b": a_log[1],
        "d_skip": 1.0 + nrm(ks[12], (DEPTH, SSD_HEADS), 0.02),
        "ssd_norm": 1.0 + nrm(ks[13], (DEPTH, SSD_INNER), 0.02),
        "w_out": nrm(ks[14], (DEPTH, MIX_W, D_MODEL), MIX_W ** -0.5),
        "g_ffn": 1.0 + nrm(ks[15], (DEPTH, D_MODEL), 0.02),
        "w_gate": nrm(ks[16], (DEPTH, D_MODEL, D_FF), D_MODEL ** -0.5),
        "w_up": nrm(ks[17], (DEPTH, D_MODEL, D_FF), D_MODEL ** -0.5),
        "w_down": nrm(ks[18], (DEPTH, D_FF, D_MODEL), D_FF ** -0.5),
    }


def reference(x_prompt, x_sample, meta_tokens, g_mix, w_in, q_norm, k_norm, rpb, conv_w, conv_b,
              dt_bias_f, dt_bias_b, a_log_f, a_log_b, d_skip, ssd_norm, w_out, g_ffn, w_gate, w_up, w_down):
    y_prompt = run_trunk(x_prompt, meta_tokens, g_mix, w_in, q_norm, k_norm, rpb, conv_w, conv_b,
                         dt_bias_f, dt_bias_b, a_log_f, a_log_b, d_skip, ssd_norm, w_out, g_ffn,
                         w_gate, w_up, w_down)
    y_sample = run_trunk(x_sample, meta_tokens, g_mix, w_in, q_norm, k_norm, rpb, conv_w, conv_b,
                         dt_bias_f, dt_bias_b, a_log_f, a_log_b, d_skip, ssd_norm, w_out, g_ffn,
                         w_gate, w_up, w_down)
    return (y_prompt, y_sample)
```

```python
import functools

import numpy as np
import jax
import jax.numpy as jnp
from jax import lax
from jax.experimental import pallas as pl
from jax.experimental.pallas import tpu as pltpu

F32 = jnp.float32
BF16 = jnp.bfloat16

D_MODEL = 2048
N_META = 16
GRID_W = 64
WIN_ROWS = 8
WIN_COLS = 16
N_Q_HEADS = 32
N_KV_HEADS = 8
Q_PER_KV = N_Q_HEADS // N_KV_HEADS
HEAD_DIM = 64
ATTN_W = N_Q_HEADS * HEAD_DIM
KV_W = N_KV_HEADS * HEAD_DIM
SSD_INNER = 2 * D_MODEL
SSD_HEAD_DIM = 64
SSD_HEADS = SSD_INNER // SSD_HEAD_DIM
SSD_GROUPS = 8
SSD_HEADS_PER_GROUP = SSD_HEADS // SSD_GROUPS
SSD_STATE = 128
D_CONV = 4
CHUNK = 128
CONV_DIM = SSD_INNER + 2 * SSD_GROUPS * SSD_STATE
D_FF = 5632
EPS = 1e-6

GROUP_W = SSD_HEADS_PER_GROUP * SSD_HEAD_DIM
PROJ_W = ATTN_W + 2 * KV_W + SSD_INNER + CONV_DIM
COL_Z = 0
COL_XBC = SSD_INNER
COL_Q = COL_XBC + CONV_DIM
COL_K = COL_Q + ATTN_W
COL_V = COL_K + KV_W
NEG = -1e30
VMEM_LIMIT = 56 * 1024 * 1024


def _cparams(*sem):
    return pltpu.CompilerParams(dimension_semantics=sem, vmem_limit_bytes=VMEM_LIMIT)


def _dot(a, b):
    return jnp.dot(a, b, preferred_element_type=F32)


def _dot_nt(a, b):
    return lax.dot_general(a, b, (((1,), (1,)), ((), ())), preferred_element_type=F32)


def _dot_tn(a, b):
    return lax.dot_general(a, b, (((0,), (0,)), ((), ())), preferred_element_type=F32)


def _split2(x):
    hi = x.astype(BF16)
    lo = (x - hi.astype(F32)).astype(BF16)
    return hi, lo


def _split3(x):
    hi = x.astype(BF16)
    r = x - hi.astype(F32)
    mid = r.astype(BF16)
    lo = (r - mid.astype(F32)).astype(BF16)
    return hi, mid, lo


def _silu(x):
    return x / (1.0 + jnp.exp(-x))


IN_TN = 1024
N_QK_TILES = ATTN_W // IN_TN
KV_TILE = N_QK_TILES


def _head_norm(a, gain, ones_bd):
    sq = a * a
    hi, lo = _split2(sq)
    parts = []
    for c in range(a.shape[1] // 256):
        sl = slice(c * 256, (c + 1) * 256)
        parts.append(_dot(hi[:, sl], ones_bd) + _dot(lo[:, sl], ones_bd))
    ms = jnp.concatenate(parts, axis=1) * (1.0 / HEAD_DIM)
    return a * lax.rsqrt(ms + EPS) * gain


def _inproj_kernel(x_ref, g_ref, w_ref, wdt_ref, qg_ref, kg_ref, ones_ref, o_ref, dt_ref, u_ref):
    j = pl.program_id(1)

    @pl.when(j == 0)
    def _():
        x = x_ref[...]
        ms = jnp.mean(x * x, axis=-1, keepdims=True)
        u = (x * lax.rsqrt(ms + EPS) * g_ref[...]).astype(BF16)
        u_ref[...] = u
        dt_ref[...] = _dot(u, wdt_ref[...])

    acc = _dot(u_ref[...], w_ref[...])

    @pl.when(j < N_QK_TILES)
    def _():
        o_ref[...] = _head_norm(acc, qg_ref[...], ones_ref[...]).astype(BF16)

    @pl.when(j == KV_TILE)
    def _():
        kn = _head_norm(acc[:, :KV_W], kg_ref[...], ones_ref[...])
        o_ref[...] = jnp.concatenate([kn, acc[:, KV_W:]], axis=1).astype(BF16)

    @pl.when(j > KV_TILE)
    def _():
        o_ref[...] = acc.astype(BF16)


def _in_proj(x2d, g_mix, w_main, w_dt, q_gain, k_gain, ones_bd, tm):
    m = x2d.shape[0]
    n_tiles = PROJ_W // IN_TN
    dst = lambda j: jnp.where(j <= KV_TILE, j + (COL_Q // IN_TN), j - (KV_TILE + 1))
    return pl.pallas_call(
        _inproj_kernel,
        grid=(m // tm, n_tiles),
        in_specs=[
            pl.BlockSpec((tm, D_MODEL), lambda i, j: (i, 0)),
            pl.BlockSpec((1, D_MODEL), lambda i, j: (0, 0)),
            pl.BlockSpec((D_MODEL, IN_TN), lambda i, j: (0, j)),
            pl.BlockSpec((D_MODEL, 2 * SSD_HEADS), lambda i, j: (0, 0)),
            pl.BlockSpec((1, IN_TN), lambda i, j: (0, 0)),
            pl.BlockSpec((1, KV_W), lambda i, j: (0, 0)),
            pl.BlockSpec((256, 256), lambda i, j: (0, 0)),
        ],
        out_specs=[
            pl.BlockSpec((tm, IN_TN), lambda i, j: (i, dst(j))),
            pl.BlockSpec((tm, 2 * SSD_HEADS), lambda i, j: (i, 0)),
        ],
        out_shape=[
            jax.ShapeDtypeStruct((m, PROJ_W), BF16),
            jax.ShapeDtypeStruct((m, 2 * SSD_HEADS), F32),
        ],
        scratch_shapes=[pltpu.VMEM((tm, D_MODEL), BF16)],
        compiler_params=_cparams("parallel", "arbitrary"),
        name="in_proj",
    )(x2d, g_mix, w_main, w_dt, q_gain, k_gain, ones_bd)


N_BAND = WIN_ROWS * GRID_W
META_PAD = 128


def _attn_kernel(q_ref, k_ref, v_ref, bias_ref, km_ref, vm_ref, o_ref, *, rows):
    r = pl.program_id(1)
    rs = jnp.clip(r - WIN_ROWS // 2, 0, rows - WIN_ROWS)
    start = pl.multiple_of(rs * GRID_W, GRID_W)
    meta_bias = jnp.where(lax.broadcasted_iota(jnp.int32, (1, META_PAD), 1) < N_META, 0.0, NEG)
    for kv in range(N_KV_HEADS):
        hs = slice(kv * HEAD_DIM, (kv + 1) * HEAD_DIM)
        kb = k_ref[0, pl.ds(start, N_BAND), hs]
        vb = v_ref[0, pl.ds(start, N_BAND), hs]
        q4 = jnp.concatenate(
            [q_ref[0, :, (kv * Q_PER_KV + rr) * HEAD_DIM:(kv * Q_PER_KV + rr + 1) * HEAD_DIM]
             for rr in range(Q_PER_KV)], axis=0)
        s = _dot_nt(q4, kb) + bias_ref[0, kv]
        sm = _dot_nt(q4, km_ref[:, hs]) + meta_bias
        m = jnp.maximum(jnp.max(s, axis=1, keepdims=True), jnp.max(sm, axis=1, keepdims=True))
        p = jnp.exp(s - m)
        pm = jnp.exp(sm - m)
        l = jnp.sum(p, axis=1, keepdims=True) + jnp.sum(pm, axis=1, keepdims=True)
        o = _dot(p.astype(BF16), vb) + _dot(pm.astype(BF16), vm_ref[:, hs])
        o = (o / l).astype(BF16)
        for rr in range(Q_PER_KV):
            h = kv * Q_PER_KV + rr
            o_ref[0, :, h * HEAD_DIM:(h + 1) * HEAD_DIM] = o[rr * GRID_W:(rr + 1) * GRID_W]


def _attention(proj3, bias_tab, k_meta, v_meta):
    b, s, _ = proj3.shape
    rows = s // GRID_W

    def off(r):
        return r - jnp.clip(r - WIN_ROWS // 2, 0, rows - WIN_ROWS)

    return pl.pallas_call(
        functools.partial(_attn_kernel, rows=rows),
        grid=(b, rows),
        in_specs=[
            pl.BlockSpec((1, GRID_W, ATTN_W), lambda i, r: (i, r, COL_Q // ATTN_W)),
            pl.BlockSpec((1, s, KV_W), lambda i, r: (i, 0, COL_K // KV_W)),
            pl.BlockSpec((1, s, KV_W), lambda i, r: (i, 0, COL_V // KV_W)),
            pl.BlockSpec((1, N_KV_HEADS, Q_PER_KV * GRID_W, N_BAND), lambda i, r: (off(r), 0, 0, 0)),
            pl.BlockSpec((META_PAD, KV_W), lambda i, r: (0, 0)),
            pl.BlockSpec((META_PAD, KV_W), lambda i, r: (0, 0)),
        ],
        out_specs=pl.BlockSpec((1, GRID_W, ATTN_W), lambda i, r: (i, r, 0)),
        out_shape=jax.ShapeDtypeStruct((b, s, ATTN_W), BF16),
        compiler_params=_cparams("parallel", "arbitrary"),
        name="na_attention",
    )(proj3, proj3, proj3, bias_tab, k_meta, v_meta)


def _bias_table(rpb):
    c = np.arange(GRID_W)
    cs = np.clip(c - WIN_COLS // 2, 0, GRID_W - WIN_COLS)
    inwin = (c[None, :] >= cs[:, None]) & (c[None, :] < cs[:, None] + WIN_COLS)
    dc = np.clip(c[None, :] - c[:, None] + WIN_COLS - 1, 0, 2 * WIN_COLS - 2)
    dr = np.arange(WIN_ROWS)[None, :] - np.arange(WIN_ROWS)[:, None] + WIN_ROWS - 1
    t = rpb.astype(F32)[:, dr][:, :, :, dc]
    t = jnp.where(inwin[None, None, None], t, NEG)
    t = t.reshape(N_KV_HEADS, Q_PER_KV, WIN_ROWS, WIN_ROWS, GRID_W, GRID_W)
    return t.transpose(2, 0, 1, 4, 3, 5).reshape(WIN_ROWS, N_KV_HEADS, Q_PER_KV * GRID_W, N_BAND)


HALO = 16


def _conv_kernel(cur_ref, prev_ref, next_ref, lh_ref, rh_ref, w_ref, b_ref, o_ref):
    i = pl.program_id(1)
    last = pl.num_programs(1) - 1
    x = cur_ref[0].astype(F32)
    tc = x.shape[0]
    p = jnp.where(i == 0, lh_ref[0], prev_ref[0]).astype(F32)
    n = jnp.where(i == last, rh_ref[0], next_ref[0]).astype(F32)
    row = lax.broadcasted_iota(jnp.int32, x.shape, 0)
    xm1 = jnp.where(row == 0, p[HALO - 1:HALO], pltpu.roll(x, 1, 0))
    xm2 = jnp.where(row == 0, p[HALO - 2:HALO - 1], jnp.where(row == 1, p[HALO - 1:HALO], pltpu.roll(x, 2, 0)))
    xp1 = jnp.where(row == tc - 1, n[0:1], pltpu.roll(x, tc - 1, 0))
    w = w_ref[...]
    y = w[0:1] * xm2 + w[1:2] * xm1 + w[2:3] * x + w[3:4] * xp1 + b_ref[...]
    o_ref[0] = _silu(y).astype(BF16)


def _conv(src, col0, lh, rh, conv_w, conv_b, tc, cw, shared):
    b = lh.shape[0]
    s = src.shape[1]
    hb = tc // HALO
    nhb = s // HALO
    c0 = col0 // cw
    bi = (lambda i: 0) if shared else (lambda i: i)
    return pl.pallas_call(
        _conv_kernel,
        grid=(b, s // tc, CONV_DIM // cw),
        in_specs=[
            pl.BlockSpec((1, tc, cw), lambda i, t, c: (bi(i), t, c0 + c)),
            pl.BlockSpec((1, HALO, cw), lambda i, t, c: (bi(i), jnp.maximum(t * hb - 1, 0), c0 + c)),
            pl.BlockSpec((1, HALO, cw), lambda i, t, c: (bi(i), jnp.minimum((t + 1) * hb, nhb - 1), c0 + c)),
            pl.BlockSpec((1, HALO, cw), lambda i, t, c: (i, 0, c)),
            pl.BlockSpec((1, HALO, cw), lambda i, t, c: (i, 0, c)),
            pl.BlockSpec((D_CONV, cw), lambda i, t, c: (0, c)),
            pl.BlockSpec((1, cw), lambda i, t, c: (0, c)),
        ],
        out_specs=pl.BlockSpec((1, tc, cw), lambda i, t, c: (i, t, c)),
        out_shape=jax.ShapeDtypeStruct((b, s, CONV_DIM), BF16),
        compiler_params=_cparams("parallel", "arbitrary", "arbitrary"),
        name="conv_silu",
    )(src, src, src, lh, rh, conv_w, conv_b)


def _softplus(x):
    return jnp.maximum(x, 0.0) + jnp.log1p(jnp.exp(-jnp.abs(x)))


def _ssd_chunk(x_ref, b_ref, c_ref, dtraw_ref, dtb_ref, alog_ref, tri_ref, exp_ref, state_ref,
               *, reverse, n_valid):
    L = x_ref.shape[1]
    off = SSD_HEADS if reverse else 0
    dt = _softplus(dtraw_ref[0] + dtb_ref[...])
    if n_valid < L:
        dt = jnp.where(lax.broadcasted_iota(jnp.int32, dt.shape, 0) < n_valid, dt, 0.0)
    da = dt * (-jnp.exp(alog_ref[...]))
    tri = tri_ref[...]
    acs = sum(_dot(tri, piece) for piece in _split3(da))
    tot = acs[0:1] if reverse else acs[L - 1:L]
    acs_t = acs.T
    dt_t = dt.T
    expand = exp_ref[...]

    def widen(v):
        hi, lo = _split2(v)
        return _dot(hi, expand) + _dot(lo, expand)

    e_out = widen(jnp.exp(acs))
    e_in = widen(jnp.exp(tot - acs) * dt)
    e_tot = e_out[0:1] if reverse else e_out[L - 1:L]
    xw = (x_ref[0].astype(F32) * e_in).astype(BF16)

    li = lax.broadcasted_iota(jnp.int32, (L, L), 0)
    si = lax.broadcasted_iota(jnp.int32, (L, L), 1)
    keep = (si >= li) if reverse else (si <= li)
    lane = lax.broadcasted_iota(jnp.int32, (L, 2 * SSD_HEAD_DIM), 1)
    left = lane < SSD_HEAD_DIM

    ys = []
    for g in range(SSD_GROUPS):
        gs = slice(g * GROUP_W, (g + 1) * GROUP_W)
        ns = slice(g * SSD_STATE, (g + 1) * SSD_STATE)
        bg = b_ref[0, :, ns]
        cg = c_ref[0, :, ns]
        cb = _dot_nt(cg, bg)
        st = state_ref[g]
        y_inter = _dot(cg, st.astype(BF16)) * e_out[:, gs]
        state_ref[g] = st * e_tot[:, gs] + _dot_tn(bg, xw[:, gs])
        pairs = []
        for j in range(SSD_HEADS_PER_GROUP // 2):
            ms = []
            for h in (off + g * SSD_HEADS_PER_GROUP + 2 * j, off + g * SSD_HEADS_PER_GROUP + 2 * j + 1):
                seg = acs[:, h:h + 1] - acs_t[h:h + 1, :]
                lm = jnp.exp(jnp.where(keep, seg, -jnp.inf))
                ms.append((cb * lm * dt_t[h:h + 1, :]).astype(BF16))
            xp = x_ref[0, :, g * GROUP_W + j * 128:g * GROUP_W + (j + 1) * 128]
            zero = jnp.zeros_like(xp)
            xbd = jnp.concatenate([jnp.where(left, xp, zero), jnp.where(left, zero, xp)], axis=0)
            pairs.append(_dot(jnp.concatenate(ms, axis=1), xbd))
        ys.append(jnp.concatenate(pairs, axis=1) + y_inter)
    return jnp.concatenate(ys, axis=1)


def _ssd_fwd_kernel(x_ref, b_ref, c_ref, dtraw_ref, dtb_ref, alog_ref, tri_ref, exp_ref, init_ref,
                    y_ref, fin_ref, state_ref, *, n_valid):
    c = pl.program_id(1)

    @pl.when(c == 0)
    def _():
        state_ref[...] = init_ref[0]

    y = _ssd_chunk(x_ref, b_ref, c_ref, dtraw_ref, dtb_ref, alog_ref, tri_ref, exp_ref, state_ref,
                   reverse=False, n_valid=n_valid)
    y_ref[0] = y

    @pl.when(c == pl.num_programs(1) - 1)
    def _():
        fin_ref[0] = state_ref[...]


def _ssd_bwd_kernel(x_ref, b_ref, c_ref, dtraw_ref, dtb_ref, alog_ref, tri_ref, exp_ref,
                    yf_ref, z_ref, dskip_ref, nw_ref, o_ref, state_ref):
    c = pl.program_id(1)

    @pl.when(c == 0)
    def _():
        state_ref[...] = jnp.zeros_like(state_ref)

    L = x_ref.shape[1]
    y = _ssd_chunk(x_ref, b_ref, c_ref, dtraw_ref, dtb_ref, alog_ref, tri_ref, exp_ref, state_ref,
                   reverse=True, n_valid=L)
    y = y + yf_ref[0] + x_ref[0].astype(F32) * dskip_ref[...]
    y = y * _silu(z_ref[0].astype(F32))
    outs = []
    for g in range(SSD_GROUPS):
        yg = y[:, g * GROUP_W:(g + 1) * GROUP_W]
        ms = jnp.mean(yg * yg, axis=-1, keepdims=True)
        outs.append(yg * lax.rsqrt(ms + EPS))
    o_ref[0] = (jnp.concatenate(outs, axis=1) * nw_ref[...]).astype(BF16)


def _ssd_common_specs(L, cmap):
    nb = SSD_GROUPS * SSD_STATE
    return [
        pl.BlockSpec((1, L, SSD_INNER), lambda i, c: (i, cmap(c), 0)),
        pl.BlockSpec((1, L, nb), lambda i, c: (i, cmap(c), SSD_INNER // nb)),
        pl.BlockSpec((1, L, nb), lambda i, c: (i, cmap(c), SSD_INNER // nb + 1)),
    ]


def _const_spec(shape):
    nd = len(shape)
    return pl.BlockSpec(shape, lambda i, c: (0,) * nd)


def _ssd_forward(xc, dt_raw, dt_bias, a_log, tri, expand, init, n_valid=CHUNK):
    b, s, _ = xc.shape
    L = CHUNK
    st_shape = (SSD_GROUPS, SSD_STATE, GROUP_W)
    return pl.pallas_call(
        functools.partial(_ssd_fwd_kernel, n_valid=n_valid),
        grid=(b, s // L),
        in_specs=_ssd_common_specs(L, lambda c: c) + [
            pl.BlockSpec((1, L, 2 * SSD_HEADS), lambda i, c: (i, c, 0)),
            _const_spec((1, 2 * SSD_HEADS)), _const_spec((1, 2 * SSD_HEADS)),
            _const_spec((L, L)), _const_spec((2 * SSD_HEADS, SSD_INNER)),
            pl.BlockSpec((1,) + st_shape, lambda i, c: (i, 0, 0, 0)),
        ],
        out_specs=[
            pl.BlockSpec((1, L, SSD_INNER), lambda i, c: (i, c, 0)),
            pl.BlockSpec((1,) + st_shape, lambda i, c: (i, 0, 0, 0)),
        ],
        out_shape=[
            jax.ShapeDtypeStruct((b, s, SSD_INNER), F32),
            jax.ShapeDtypeStruct((b,) + st_shape, F32),
        ],
        scratch_shapes=[pltpu.VMEM(st_shape, F32)],
        compiler_params=_cparams("parallel", "arbitrary"),
        name="ssd_forward",
    )(xc, xc, xc, dt_raw, dt_bias, a_log, tri, expand, init)


def _ssd_backward(xc, dt_raw, dt_bias, a_log, tri, expand, y_f, proj3, d_skip, norm_w):
    b, s, _ = xc.shape
    L = CHUNK
    nc = s // L
    rev = lambda c: nc - 1 - c
    return pl.pallas_call(
        _ssd_bwd_kernel,
        grid=(b, nc),
        in_specs=_ssd_common_specs(L, rev) + [
            pl.BlockSpec((1, L, 2 * SSD_HEADS), lambda i, c: (i, rev(c), 0)),
            _const_spec((1, 2 * SSD_HEADS)), _const_spec((1, 2 * SSD_HEADS)),
            _const_spec((L, L)), _const_spec((2 * SSD_HEADS, SSD_INNER)),
            pl.BlockSpec((1, L, SSD_INNER), lambda i, c: (i, rev(c), 0)),
            pl.BlockSpec((1, L, SSD_INNER), lambda i, c: (i, rev(c), COL_Z // SSD_INNER)),
            _const_spec((1, SSD_INNER)), _const_spec((1, SSD_INNER)),
        ],
        out_specs=pl.BlockSpec((1, L, SSD_INNER), lambda i, c: (i, rev(c), 0)),
        out_shape=jax.ShapeDtypeStruct((b, s, SSD_INNER), BF16),
        scratch_shapes=[pltpu.VMEM((SSD_GROUPS, SSD_STATE, GROUP_W), F32)],
        compiler_params=_cparams("parallel", "arbitrary"),
        name="ssd_backward",
    )(xc, xc, xc, dt_raw, dt_bias, a_log, tri, expand, y_f, proj3, d_skip, norm_w)


def _outproj_kernel(a_ref, s_ref, wa_ref, ws_ref, x_ref, o_ref):
    o_ref[...] = x_ref[...] + _dot(a_ref[...], wa_ref[...]) + _dot(s_ref[...], ws_ref[...])


def _out_proj(attn2, ssd2, w_a, w_s, x2d, tm, tn):
    m = x2d.shape[0]
    return pl.pallas_call(
        _outproj_kernel,
        grid=(D_MODEL // tn, m // tm),
        in_specs=[
            pl.BlockSpec((tm, ATTN_W), lambda j, i: (i, 0)),
            pl.BlockSpec((tm, SSD_INNER), lambda j, i: (i, 0)),
            pl.BlockSpec((ATTN_W, tn), lambda j, i: (0, j)),
            pl.BlockSpec((SSD_INNER, tn), lambda j, i: (0, j)),
            pl.BlockSpec((tm, tn), lambda j, i: (i, j)),
        ],
        out_specs=pl.BlockSpec((tm, tn), lambda j, i: (i, j)),
        out_shape=jax.ShapeDtypeStruct((m, D_MODEL), F32),
        compiler_params=_cparams("parallel", "parallel"),
        name="out_proj",
    )(attn2, ssd2, w_a, w_s, x2d)


def _ffn_kernel(h_ref, g_ref, wg_ref, wu_ref, wd_ref, o_ref, f_ref, acc_ref):
    j = pl.program_id(1)

    @pl.when(j == 0)
    def _():
        h = h_ref[...]
        ms = jnp.mean(h * h, axis=-1, keepdims=True)
        f_ref[...] = (h * lax.rsqrt(ms + EPS) * g_ref[...]).astype(BF16)
        acc_ref[...] = jnp.zeros_like(acc_ref)

    f = f_ref[...]
    gate = _dot(f, wg_ref[...])
    up = _dot(f, wu_ref[...])
    acc_ref[...] += _dot((_silu(gate) * up).astype(BF16), wd_ref[...])

    @pl.when(j == pl.num_programs(1) - 1)
    def _():
        o_ref[...] = h_ref[...] + acc_ref[...]


def _ffn(h2d, g_ffn, w_gate, w_up, w_down, tm, tf):
    m = h2d.shape[0]
    return pl.pallas_call(
        _ffn_kernel,
        grid=(m // tm, D_FF // tf),
        in_specs=[
            pl.BlockSpec((tm, D_MODEL), lambda i, j: (i, 0)),
            pl.BlockSpec((1, D_MODEL), lambda i, j: (0, 0)),
            pl.BlockSpec((D_MODEL, tf), lambda i, j: (0, j)),
            pl.BlockSpec((D_MODEL, tf), lambda i, j: (0, j)),
            pl.BlockSpec((tf, D_MODEL), lambda i, j: (j, 0)),
        ],
        out_specs=pl.BlockSpec((tm, D_MODEL), lambda i, j: (i, 0)),
        out_shape=jax.ShapeDtypeStruct((m, D_MODEL), F32),
        scratch_shapes=[pltpu.VMEM((tm, D_MODEL), BF16), pltpu.VMEM((tm, D_MODEL), F32)],
        compiler_params=_cparams("parallel", "arbitrary"),
        name="ffn",
    )(h2d, g_ffn, w_gate, w_up, w_down)


_EXPAND = np.kron(np.eye(SSD_HEADS), np.ones((1, SSD_HEAD_DIM)))


def _prepare(meta_tokens, g_mix, w_in, q_norm, k_norm, rpb, conv_w, conv_b, dt_bias_f, dt_bias_b,
             a_log_f, a_log_b, d_skip, ssd_norm, w_out, g_ffn, w_gate, w_up, w_down):
    l = 0
    w_in_b = w_in[l].astype(BF16)
    p = dict(
        g_mix=g_mix[l].reshape(1, D_MODEL),
        w_main=w_in_b[:, :PROJ_W],
        w_dt=w_in_b[:, PROJ_W:],
        q_gain=jnp.tile(q_norm[l].astype(F32) * HEAD_DIM ** -0.5, IN_TN // HEAD_DIM).reshape(1, IN_TN),
        k_gain=jnp.tile(k_norm[l].astype(F32), N_KV_HEADS).reshape(1, KV_W),
        ones_bd=jnp.asarray(np.kron(np.eye(256 // HEAD_DIM), np.ones((HEAD_DIM, HEAD_DIM))), BF16),
        bias_tab=_bias_table(rpb[l]),
        conv_w=conv_w[l].astype(F32),
        conv_b=conv_b[l].astype(F32).reshape(1, CONV_DIM),
        dt_bias=jnp.concatenate([dt_bias_f[l], dt_bias_b[l]]).astype(F32).reshape(1, 2 * SSD_HEADS),
        a_log=jnp.concatenate([a_log_f[l], a_log_b[l]]).astype(F32).reshape(1, 2 * SSD_HEADS),
        d_skip=jnp.repeat(d_skip[l].astype(F32), SSD_HEAD_DIM).reshape(1, SSD_INNER),
        ssd_norm=ssd_norm[l].astype(F32).reshape(1, SSD_INNER),
        tril=jnp.asarray(np.tril(np.ones((CHUNK, CHUNK))), BF16),
        triu=jnp.asarray(np.triu(np.ones((CHUNK, CHUNK))), BF16),
        expand_f=jnp.asarray(np.concatenate([_EXPAND, 0 * _EXPAND]), BF16),
        expand_b=jnp.asarray(np.concatenate([0 * _EXPAND, _EXPAND]), BF16),
        w_out_a=w_out[l, :ATTN_W].astype(BF16),
        w_out_s=w_out[l, ATTN_W:].astype(BF16),
        g_ffn=g_ffn[l].reshape(1, D_MODEL),
        w_gate=w_gate[l].astype(BF16),
        w_up=w_up[l].astype(BF16),
        w_down=w_down[l].astype(BF16),
    )
    proj_m, dt_m = _in_proj(meta_tokens.astype(F32), p["g_mix"], p["w_main"], p["w_dt"], p["q_gain"],
                            p["k_gain"], p["ones_bd"], tm=N_META)
    pad = ((0, META_PAD - N_META), (0, 0))
    p["k_meta"] = jnp.pad(proj_m[:, COL_K:COL_K + KV_W], pad)
    p["v_meta"] = jnp.pad(proj_m[:, COL_V:COL_V + KV_W], pad)
    p["proj_meta"] = proj_m
    p["dt_meta"] = dt_m
    return p


def _trunk(x, p):
    b, s, _ = x.shape
    m = b * s
    x2d = x.reshape(m, D_MODEL)
    proj, dt_raw = _in_proj(x2d, p["g_mix"], p["w_main"], p["w_dt"], p["q_gain"], p["k_gain"],
                            p["ones_bd"], tm=512)
    proj3 = proj.reshape(b, s, PROJ_W)
    dt3 = dt_raw.reshape(b, s, 2 * SSD_HEADS)

    attn = _attention(proj3, p["bias_tab"], p["k_meta"], p["v_meta"])

    xbc_meta = p["proj_meta"][:, COL_XBC:COL_XBC + CONV_DIM]
    zeros_h = jnp.zeros((b, HALO, CONV_DIM), BF16)
    xc = _conv(proj3, COL_XBC, jnp.broadcast_to(xbc_meta[None], (b, HALO, CONV_DIM)), zeros_h,
               p["conv_w"], p["conv_b"], tc=512, cw=1024, shared=False)
    xc_meta = _conv(p["proj_meta"][None], COL_XBC, zeros_h, proj3[:, :HALO, COL_XBC:COL_XBC + CONV_DIM],
                    p["conv_w"], p["conv_b"], tc=N_META, cw=1024, shared=True)

    xc_meta = jnp.pad(xc_meta, ((0, 0), (0, CHUNK - N_META), (0, 0)))
    dt_meta = jnp.broadcast_to(jnp.pad(p["dt_meta"], ((0, CHUNK - N_META), (0, 0)))[None],
                               (b, CHUNK, 2 * SSD_HEADS))
    zero_state = jnp.zeros((b, SSD_GROUPS, SSD_STATE, GROUP_W), F32)
    _, state0 = _ssd_forward(xc_meta, dt_meta, p["dt_bias"], p["a_log"], p["tril"], p["expand_f"],
                             zero_state, n_valid=N_META)
    y_f, _ = _ssd_forward(xc, dt3, p["dt_bias"], p["a_log"], p["tril"], p["expand_f"], state0)
    ssd = _ssd_backward(xc, dt3, p["dt_bias"], p["a_log"], p["triu"], p["expand_b"], y_f, proj3,
                        p["d_skip"], p["ssd_norm"])

    h1 = _out_proj(attn.reshape(m, ATTN_W), ssd.reshape(m, SSD_INNER), p["w_out_a"], p["w_out_s"], x2d,
                   tm=512, tn=1024)
    out = _ffn(h1, p["g_ffn"], p["w_gate"], p["w_up"], p["w_down"], tm=512, tf=512)
    return out.reshape(b, s, D_MODEL)


def kernel(x_prompt, x_sample, meta_tokens, g_mix, w_in, q_norm, k_norm, rpb, conv_w, conv_b, dt_bias_f, dt_bias_b, a_log_f, a_log_b, d_skip, ssd_norm, w_out, g_ffn, w_gate, w_up, w_down):
    p = _prepare(meta_tokens, g_mix, w_in, q_norm, k_norm, rpb, conv_w, conv_b, dt_bias_f, dt_bias_b,
                 a_log_f, a_log_b, d_skip, ssd_norm, w_out, g_ffn, w_gate, w_up, w_down)
    return (_trunk(x_prompt, p), _trunk(x_sample, p))
```

```python
import functools

import numpy as np
import jax
import jax.numpy as jnp
from jax import lax
from jax.experimental import pallas as pl
from jax.experimental.pallas import tpu as pltpu

F32 = jnp.float32
BF16 = jnp.bfloat16

D_MODEL = 2048
N_META = 16
GRID_W = 64
WIN_ROWS = 8
WIN_COLS = 16
N_Q_HEADS = 32
N_KV_HEADS = 8
Q_PER_KV = N_Q_HEADS // N_KV_HEADS
HEAD_DIM = 64
ATTN_W = N_Q_HEADS * HEAD_DIM
KV_W = N_KV_HEADS * HEAD_DIM
SSD_INNER = 2 * D_MODEL
SSD_HEAD_DIM = 64
SSD_HEADS = SSD_INNER // SSD_HEAD_DIM
SSD_GROUPS = 8
SSD_HEADS_PER_GROUP = SSD_HEADS // SSD_GROUPS
SSD_STATE = 128
D_CONV = 4
CHUNK = 128
CONV_DIM = SSD_INNER + 2 * SSD_GROUPS * SSD_STATE
D_FF = 5632
EPS = 1e-6

GROUP_W = SSD_HEADS_PER_GROUP * SSD_HEAD_DIM
PROJ_W = ATTN_W + 2 * KV_W + SSD_INNER + CONV_DIM
COL_Z = 0
COL_XBC = SSD_INNER
COL_Q = COL_XBC + CONV_DIM
COL_K = COL_Q + ATTN_W
COL_V = COL_K + KV_W
NEG = -1e30
VMEM_LIMIT = 56 * 1024 * 1024


def _cparams(*sem):
    return pltpu.CompilerParams(dimension_semantics=sem, vmem_limit_bytes=VMEM_LIMIT)


def _dot(a, b):
    return jnp.dot(a, b, preferred_element_type=F32)


def _dot_nt(a, b):
    return lax.dot_general(a, b, (((1,), (1,)), ((), ())), preferred_element_type=F32)


def _dot_tn(a, b):
    return lax.dot_general(a, b, (((0,), (0,)), ((), ())), preferred_element_type=F32)


def _split2(x):
    hi = x.astype(BF16)
    lo = (x - hi.astype(F32)).astype(BF16)
    return hi, lo


def _split3(x):
    hi = x.astype(BF16)
    r = x - hi.astype(F32)
    mid = r.astype(BF16)
    lo = (r - mid.astype(F32)).astype(BF16)
    return hi, mid, lo


def _silu(x):
    return x / (1.0 + jnp.exp(-x))


IN_TN = 1024
N_QK_TILES = ATTN_W // IN_TN
KV_TILE = N_QK_TILES


def _head_norm(a, gain, ones_bd):
    sq = a * a
    hi, lo = _split2(sq)
    parts = []
    for c in range(a.shape[1] // 256):
        sl = slice(c * 256, (c + 1) * 256)
        parts.append(_dot(hi[:, sl], ones_bd) + _dot(lo[:, sl], ones_bd))
    ms = jnp.concatenate(parts, axis=1) * (1.0 / HEAD_DIM)
    return a * lax.rsqrt(ms + EPS) * gain


def _inproj_kernel(x_ref, g_ref, w_ref, wdt_ref, qg_ref, kg_ref, ones_ref, o_ref, dt_ref, u_ref):
    j = pl.program_id(1)

    @pl.when(j == 0)
    def _():
        x = x_ref[...]
        ms = jnp.mean(x * x, axis=-1, keepdims=True)
        u = (x * lax.rsqrt(ms + EPS) * g_ref[...]).astype(BF16)
        u_ref[...] = u
        dt_ref[...] = _dot(u, wdt_ref[...])

    acc = _dot(u_ref[...], w_ref[...])

    @pl.when(j < N_QK_TILES)
    def _():
        o_ref[...] = _head_norm(acc, qg_ref[...], ones_ref[...]).astype(BF16)

    @pl.when(j == KV_TILE)
    def _():
        kn = _head_norm(acc[:, :KV_W], kg_ref[...], ones_ref[...])
        o_ref[...] = jnp.concatenate([kn, acc[:, KV_W:]], axis=1).astype(BF16)

    @pl.when(j > KV_TILE)
    def _():
        o_ref[...] = acc.astype(BF16)


def _in_proj(x2d, g_mix, w_in, q_gain, k_gain, ones_bd, tm):
    m = x2d.shape[0]
    n_tiles = PROJ_W // IN_TN
    dst = lambda j: jnp.where(j <= KV_TILE, j + (COL_Q // IN_TN), j - (KV_TILE + 1))
    return pl.pallas_call(
        _inproj_kernel,
        grid=(m // tm, n_tiles),
        in_specs=[
            pl.BlockSpec((tm, D_MODEL), lambda i, j: (i, 0)),
            pl.BlockSpec((1, D_MODEL), lambda i, j: (0, 0)),
            pl.BlockSpec((D_MODEL, IN_TN), lambda i, j: (0, j)),
            pl.BlockSpec((D_MODEL, 2 * SSD_HEADS), lambda i, j: (0, PROJ_W // (2 * SSD_HEADS))),
            pl.BlockSpec((1, IN_TN), lambda i, j: (0, 0)),
            pl.BlockSpec((1, KV_W), lambda i, j: (0, 0)),
            pl.BlockSpec((256, 256), lambda i, j: (0, 0)),
        ],
        out_specs=[
            pl.BlockSpec((tm, IN_TN), lambda i, j: (i, dst(j))),
            pl.BlockSpec((tm, 2 * SSD_HEADS), lambda i, j: (i, 0)),
        ],
        out_shape=[
            jax.ShapeDtypeStruct((m, PROJ_W), BF16),
            jax.ShapeDtypeStruct((m, 2 * SSD_HEADS), F32),
        ],
        scratch_shapes=[pltpu.VMEM((tm, D_MODEL), BF16)],
        compiler_params=_cparams("parallel", "arbitrary"),
        name="in_proj",
    )(x2d, g_mix, w_in, w_in, q_gain, k_gain, ones_bd)


N_BAND = WIN_ROWS * GRID_W
HALF_W = GRID_W // 2
META_AT = {0: GRID_W - N_META, 1: 0}


def _swap_halves(t):
    return jnp.concatenate([t[:, HEAD_DIM:], t[:, :HEAD_DIM]], axis=1)


def _with_meta(band, meta, hf):
    at = META_AT[hf]
    parts = [band[:at]] if at else []
    return jnp.concatenate(parts + [meta, band[at + N_META:]], axis=0)


def _attn_kernel(q_ref, k_ref, v_ref, bias_ref, km_ref, vm_ref, o_ref, *, rows):
    r = pl.program_id(1)
    rs = jnp.clip(r - WIN_ROWS // 2, 0, rows - WIN_ROWS)
    start = pl.multiple_of(rs * GRID_W, GRID_W)
    lo = lax.broadcasted_iota(jnp.int32, (HALF_W, 2 * HEAD_DIM), 1) < HEAD_DIM
    hi = jnp.logical_not(lo)
    ones = jnp.ones((N_BAND, 2 * HEAD_DIM), BF16)
    units = [(kp, hf) for kp in range(N_KV_HEADS // 2) for hf in range(2)]

    def scores(kp, hf):
        ps = slice(kp * 2 * HEAD_DIM, (kp + 1) * 2 * HEAD_DIM)
        kb = _with_meta(k_ref[0, pl.ds(start, N_BAND), ps], km_ref[:, ps], hf)
        tiles = []
        for par in range(2):
            keep = lo if par == 0 else hi
            for t in range(Q_PER_KV // 2):
                c0 = ((2 * kp + par) * 2 + t) * 128
                qt = q_ref[0, hf * HALF_W:(hf + 1) * HALF_W, c0:c0 + 128]
                qs = _swap_halves(qt)
                for piece in ((qt, qs) if par == 0 else (qs, qt)):
                    tiles.append(jnp.where(keep, piece, jnp.zeros_like(piece)))
        return _dot_nt(jnp.concatenate(tiles, axis=0), kb)

    s_next = scores(*units[0])
    for idx, (kp, hf) in enumerate(units):
        s = s_next + bias_ref[0, kp, hf]
        if idx + 1 < len(units):
            s_next = scores(*units[idx + 1])
        ps = slice(kp * 2 * HEAD_DIM, (kp + 1) * 2 * HEAD_DIM)
        vb = _with_meta(v_ref[0, pl.ds(start, N_BAND), ps], vm_ref[:, ps], hf)
        mx = jnp.maximum(jnp.maximum(s[:, 0:128], s[:, 128:256]), jnp.maximum(s[:, 256:384], s[:, 384:512]))
        m = jnp.max(mx, axis=1, keepdims=True)
        p = jnp.exp(s - m).astype(BF16)
        oa = _dot(p, jnp.concatenate([vb, ones], axis=1))
        o = oa[:, :128] / oa[:, 128:]
        for par in range(2):
            for t in range(Q_PER_KV // 2):
                base = (par * Q_PER_KV + 2 * t) * HALF_W
                a = o[base:base + HALF_W]
                b = o[base + HALF_W:base + 2 * HALF_W]
                if par == 0:
                    tile = jnp.where(lo, a, pltpu.roll(b, HEAD_DIM, 1))
                else:
                    tile = jnp.where(lo, pltpu.roll(a, HEAD_DIM, 1), b)
                c0 = ((2 * kp + par) * 2 + t) * 128
                o_ref[0, hf * HALF_W:(hf + 1) * HALF_W, c0:c0 + 128] = tile.astype(BF16)


def _attention(proj3, bias_tab, k_meta, v_meta):
    b, s, _ = proj3.shape
    rows = s // GRID_W

    def off(r):
        return r - jnp.clip(r - WIN_ROWS // 2, 0, rows - WIN_ROWS)

    return pl.pallas_call(
        functools.partial(_attn_kernel, rows=rows),
        grid=(b, rows),
        in_specs=[
            pl.BlockSpec((1, GRID_W, ATTN_W), lambda i, r: (i, r, COL_Q // ATTN_W)),
            pl.BlockSpec((1, s, KV_W), lambda i, r: (i, 0, COL_K // KV_W)),
            pl.BlockSpec((1, s, KV_W), lambda i, r: (i, 0, COL_V // KV_W)),
            pl.BlockSpec((1, N_KV_HEADS // 2, 2, 2 * Q_PER_KV * HALF_W, N_BAND),
                         lambda i, r: (off(r), 0, 0, 0, 0)),
            pl.BlockSpec((N_META, KV_W), lambda i, r: (0, 0)),
            pl.BlockSpec((N_META, KV_W), lambda i, r: (0, 0)),
        ],
        out_specs=pl.BlockSpec((1, GRID_W, ATTN_W), lambda i, r: (i, r, 0)),
        out_shape=jax.ShapeDtypeStruct((b, s, ATTN_W), BF16),
        compiler_params=_cparams("parallel", "arbitrary"),
        name="na_attention",
    )(proj3, proj3, proj3, bias_tab, k_meta, v_meta)


def _bias_table(rpb):
    c = np.arange(GRID_W)
    cs = np.clip(c - WIN_COLS // 2, 0, GRID_W - WIN_COLS)
    inwin = (c[None, :] >= cs[:, None]) & (c[None, :] < cs[:, None] + WIN_COLS)
    dc = np.clip(c[None, :] - c[:, None] + WIN_COLS - 1, 0, 2 * WIN_COLS - 2)
    slab = jnp.where(inwin[None, None], rpb.astype(F32)[:, :, dc], NEG)
    slab = slab.reshape(N_KV_HEADS // 2, 2 * Q_PER_KV, 2 * WIN_ROWS - 1, 2, HALF_W, GRID_W)
    slab = slab.transpose(0, 3, 1, 4, 2, 5)
    meta = np.zeros((2, WIN_ROWS, GRID_W), bool)
    for hf, at in META_AT.items():
        meta[hf, 0, at:at + N_META] = True
    tabs = []
    for off in range(WIN_ROWS):
        t = slab[:, :, :, :, WIN_ROWS - 1 - off:2 * WIN_ROWS - 1 - off, :]
        t = jnp.where(meta[None, :, None, None], 0.0, t)
        tabs.append(t.reshape(N_KV_HEADS // 2, 2, 2 * Q_PER_KV * HALF_W, N_BAND))
    return jnp.stack(tabs)


HALO = 16
SUBLANES = 8


def _conv_kernel(cur_ref, prev_ref, next_ref, lh_ref, rh_ref, w_ref, b_ref, o_ref):
    i = pl.program_id(1)
    last = pl.num_programs(1) - 1
    tc = cur_ref.shape[1]
    before = jnp.where(i == 0, lh_ref[0], prev_ref[0]).astype(F32)
    after = jnp.where(i == last, rh_ref[0], next_ref[0]).astype(F32)
    cw = cur_ref.shape[2]
    ng = tc // SUBLANES
    x = cur_ref[0].astype(F32).reshape(ng, SUBLANES, cw)
    head = before[HALO - SUBLANES:].reshape(1, SUBLANES, cw)
    tail = after[:SUBLANES].reshape(1, SUBLANES, cw)
    sub = lax.broadcasted_iota(jnp.int32, (1, SUBLANES, cw), 1)

    def delayed(d):
        r = pltpu.roll(x, d, 1)
        prev = jnp.concatenate([pltpu.roll(head, d, 1), r[:-1]], axis=0)
        return jnp.where(sub < d, prev, r)

    r = pltpu.roll(x, SUBLANES - 1, 1)
    nxt = jnp.concatenate([r[1:], pltpu.roll(tail, SUBLANES - 1, 1)], axis=0)
    ahead = jnp.where(sub == SUBLANES - 1, nxt, r)
    w = w_ref[...].reshape(D_CONV, 1, cw)
    y = (b_ref[...].reshape(1, 1, cw) + w[0:1] * delayed(2) + w[1:2] * delayed(1) + w[2:3] * x
         + w[3:4] * ahead)
    o_ref[0] = _silu(y).reshape(tc, cw).astype(BF16)


def _conv(src, col0, lh, rh, conv_w, conv_b, tc, cw, shared):
    b = lh.shape[0]
    s = src.shape[1]
    hb = tc // HALO
    nhb = s // HALO
    c0 = col0 // cw
    bi = (lambda i: 0) if shared else (lambda i: i)
    return pl.pallas_call(
        _conv_kernel,
        grid=(b, s // tc, CONV_DIM // cw),
        in_specs=[
            pl.BlockSpec((1, tc, cw), lambda i, t, c: (bi(i), t, c0 + c)),
            pl.BlockSpec((1, HALO, cw), lambda i, t, c: (bi(i), jnp.maximum(t * hb - 1, 0), c0 + c)),
            pl.BlockSpec((1, HALO, cw), lambda i, t, c: (bi(i), jnp.minimum((t + 1) * hb, nhb - 1), c0 + c)),
            pl.BlockSpec((1, HALO, cw), lambda i, t, c: (i, 0, c)),
            pl.BlockSpec((1, HALO, cw), lambda i, t, c: (i, 0, c)),
            pl.BlockSpec((D_CONV, cw), lambda i, t, c: (0, c)),
            pl.BlockSpec((1, cw), lambda i, t, c: (0, c)),
        ],
        out_specs=pl.BlockSpec((1, tc, cw), lambda i, t, c: (i, t, c)),
        out_shape=jax.ShapeDtypeStruct((b, s, CONV_DIM), BF16),
        compiler_params=_cparams("parallel", "arbitrary", "arbitrary"),
        name="conv_silu",
    )(src, src, src, lh, rh, conv_w, conv_b)


def _softplus(x):
    return jnp.maximum(x, 0.0) + jnp.log1p(jnp.exp(-jnp.abs(x)))


def _ssd_chunk(x_ref, b_ref, c_ref, dtraw_ref, dtb_ref, alog_ref, tri_ref, exp_ref, state_ref,
               *, reverse, n_valid):
    L = x_ref.shape[1]
    off = SSD_HEADS if reverse else 0
    dt = _softplus(dtraw_ref[0] + dtb_ref[...])
    if n_valid < L:
        dt = jnp.where(lax.broadcasted_iota(jnp.int32, dt.shape, 0) < n_valid, dt, 0.0)
    da = dt * (-jnp.exp(alog_ref[...]))
    tri = tri_ref[...]
    acs = sum(_dot(tri, piece) for piece in _split3(da))
    tot = acs[0:1] if reverse else acs[L - 1:L]
    src_t = (acs - jnp.log(dt)).T
    expand = exp_ref[...]

    def widen(v):
        return _dot(v.astype(BF16), expand)

    e_out = widen(jnp.exp(acs))
    e_in = widen(jnp.exp(tot - acs) * dt)
    e_tot = e_out[0:1] if reverse else e_out[L - 1:L]
    xw = (x_ref[0].astype(F32) * e_in).astype(BF16)

    li = lax.broadcasted_iota(jnp.int32, (L, L), 0)
    si = lax.broadcasted_iota(jnp.int32, (L, L), 1)
    keep = (si >= li) if reverse else (si <= li)
    lane = lax.broadcasted_iota(jnp.int32, (L, 2 * SSD_HEAD_DIM), 1)
    left = lane < SSD_HEAD_DIM

    ys = []
    for g in range(SSD_GROUPS):
        gs = slice(g * GROUP_W, (g + 1) * GROUP_W)
        ns = slice(g * SSD_STATE, (g + 1) * SSD_STATE)
        bg = b_ref[0, :, ns]
        cg = c_ref[0, :, ns]
        cb = _dot_nt(cg, bg)
        st = state_ref[g]
        y_inter = _dot(cg, st.astype(BF16)) * e_out[:, gs]
        state_ref[g] = st * e_tot[:, gs] + _dot_tn(bg, xw[:, gs])
        pairs = []
        for j in range(SSD_HEADS_PER_GROUP // 2):
            ms = []
            for h in (off + g * SSD_HEADS_PER_GROUP + 2 * j, off + g * SSD_HEADS_PER_GROUP + 2 * j + 1):
                seg = acs[:, h:h + 1] - src_t[h:h + 1, :]
                ms.append((cb * jnp.exp(jnp.where(keep, seg, -jnp.inf))).astype(BF16))
            xp = x_ref[0, :, g * GROUP_W + j * 128:g * GROUP_W + (j + 1) * 128]
            zero = jnp.zeros_like(xp)
            xbd = jnp.concatenate([jnp.where(left, xp, zero), jnp.where(left, zero, xp)], axis=0)
            pairs.append(_dot(jnp.concatenate(ms, axis=1), xbd))
        ys.append(jnp.concatenate(pairs, axis=1) + y_inter)
    return jnp.concatenate(ys, axis=1)


def _ssd_fwd_kernel(x_ref, b_ref, c_ref, dtraw_ref, dtb_ref, alog_ref, tri_ref, exp_ref, init_ref,
                    y_ref, fin_ref, state_ref, *, n_valid):
    c = pl.program_id(1)

    @pl.when(c == 0)
    def _():
        state_ref[...] = init_ref[0]

    y = _ssd_chunk(x_ref, b_ref, c_ref, dtraw_ref, dtb_ref, alog_ref, tri_ref, exp_ref, state_ref,
                   reverse=False, n_valid=n_valid)
    y_ref[0] = y

    @pl.when(c == pl.num_programs(1) - 1)
    def _():
        fin_ref[0] = state_ref[...]


def _ssd_bwd_kernel(x_ref, b_ref, c_ref, dtraw_ref, dtb_ref, alog_ref, tri_ref, exp_ref,
                    yf_ref, z_ref, dskip_ref, nw_ref, o_ref, state_ref):
    c = pl.program_id(1)

    @pl.when(c == 0)
    def _():
        state_ref[...] = jnp.zeros_like(state_ref)

    L = x_ref.shape[1]
    y = _ssd_chunk(x_ref, b_ref, c_ref, dtraw_ref, dtb_ref, alog_ref, tri_ref, exp_ref, state_ref,
                   reverse=True, n_valid=L)
    y = y + yf_ref[0] + x_ref[0].astype(F32) * dskip_ref[...]
    y = y * _silu(z_ref[0].astype(F32))
    outs = []
    for g in range(SSD_GROUPS):
        yg = y[:, g * GROUP_W:(g + 1) * GROUP_W]
        ms = jnp.mean(yg * yg, axis=-1, keepdims=True)
        outs.append(yg * lax.rsqrt(ms + EPS))
    o_ref[0] = (jnp.concatenate(outs, axis=1) * nw_ref[...]).astype(BF16)


def _ssd_common_specs(L, cmap):
    nb = SSD_GROUPS * SSD_STATE
    return [
        pl.BlockSpec((1, L, SSD_INNER), lambda i, c: (i, cmap(c), 0)),
        pl.BlockSpec((1, L, nb), lambda i, c: (i, cmap(c), SSD_INNER // nb)),
        pl.BlockSpec((1, L, nb), lambda i, c: (i, cmap(c), SSD_INNER // nb + 1)),
    ]


def _const_spec(shape):
    nd = len(shape)
    return pl.BlockSpec(shape, lambda i, c: (0,) * nd)


def _ssd_forward(xc, dt_raw, dt_bias, a_log, tri, expand, init, n_valid=CHUNK):
    b, s, _ = xc.shape
    L = CHUNK
    st_shape = (SSD_GROUPS, SSD_STATE, GROUP_W)
    return pl.pallas_call(
        functools.partial(_ssd_fwd_kernel, n_valid=n_valid),
        grid=(b, s // L),
        in_specs=_ssd_common_specs(L, lambda c: c) + [
            pl.BlockSpec((1, L, 2 * SSD_HEADS), lambda i, c: (i, c, 0)),
            _const_spec((1, 2 * SSD_HEADS)), _const_spec((1, 2 * SSD_HEADS)),
            _const_spec((L, L)), _const_spec((2 * SSD_HEADS, SSD_INNER)),
            pl.BlockSpec((1,) + st_shape, lambda i, c: (i, 0, 0, 0)),
        ],
        out_specs=[
            pl.BlockSpec((1, L, SSD_INNER), lambda i, c: (i, c, 0)),
            pl.BlockSpec((1,) + st_shape, lambda i, c: (i, 0, 0, 0)),
        ],
        out_shape=[
            jax.ShapeDtypeStruct((b, s, SSD_INNER), F32),
            jax.ShapeDtypeStruct((b,) + st_shape, F32),
        ],
        scratch_shapes=[pltpu.VMEM(st_shape, F32)],
        compiler_params=_cparams("parallel", "arbitrary"),
        name="ssd_forward",
    )(xc, xc, xc, dt_raw, dt_bias, a_log, tri, expand, init)


def _ssd_backward(xc, dt_raw, dt_bias, a_log, tri, expand, y_f, proj3, d_skip, norm_w):
    b, s, _ = xc.shape
    L = CHUNK
    nc = s // L
    rev = lambda c: nc - 1 - c
    return pl.pallas_call(
        _ssd_bwd_kernel,
        grid=(b, nc),
        in_specs=_ssd_common_specs(L, rev) + [
            pl.BlockSpec((1, L, 2 * SSD_HEADS), lambda i, c: (i, rev(c), 0)),
            _const_spec((1, 2 * SSD_HEADS)), _const_spec((1, 2 * SSD_HEADS)),
            _const_spec((L, L)), _const_spec((2 * SSD_HEADS, SSD_INNER)),
            pl.BlockSpec((1, L, SSD_INNER), lambda i, c: (i, rev(c), 0)),
            pl.BlockSpec((1, L, SSD_INNER), lambda i, c: (i, rev(c), COL_Z // SSD_INNER)),
            _const_spec((1, SSD_INNER)), _const_spec((1, SSD_INNER)),
        ],
        out_specs=pl.BlockSpec((1, L, SSD_INNER), lambda i, c: (i, rev(c), 0)),
        out_shape=jax.ShapeDtypeStruct((b, s, SSD_INNER), BF16),
        scratch_shapes=[pltpu.VMEM((SSD_GROUPS, SSD_STATE, GROUP_W), F32)],
        compiler_params=_cparams("parallel", "arbitrary"),
        name="ssd_backward",
    )(xc, xc, xc, dt_raw, dt_bias, a_log, tri, expand, y_f, proj3, d_skip, norm_w)


def _outproj_kernel(a_ref, s_ref, wa_ref, ws_ref, x_ref, o_ref):
    o_ref[...] = x_ref[...] + _dot(a_ref[...], wa_ref[...]) + _dot(s_ref[...], ws_ref[...])


def _out_proj(attn2, ssd2, w_a, w_s, x2d, tm, tn):
    m = x2d.shape[0]
    return pl.pallas_call(
        _outproj_kernel,
        grid=(D_MODEL // tn, m // tm),
        in_specs=[
            pl.BlockSpec((tm, ATTN_W), lambda j, i: (i, 0)),
            pl.BlockSpec((tm, SSD_INNER), lambda j, i: (i, 0)),
            pl.BlockSpec((ATTN_W, tn), lambda j, i: (0, j)),
            pl.BlockSpec((SSD_INNER, tn), lambda j, i: (0, j)),
            pl.BlockSpec((tm, tn), lambda j, i: (i, j)),
        ],
        out_specs=pl.BlockSpec((tm, tn), lambda j, i: (i, j)),
        out_shape=jax.ShapeDtypeStruct((m, D_MODEL), F32),
        compiler_params=_cparams("parallel", "parallel"),
        name="out_proj",
    )(attn2, ssd2, w_a, w_s, x2d)


def _ffn_kernel(h_ref, g_ref, wg_ref, wu_ref, wd_ref, o_ref, f_ref, acc_ref):
    j = pl.program_id(1)

    @pl.when(j == 0)
    def _():
        h = h_ref[...]
        ms = jnp.mean(h * h, axis=-1, keepdims=True)
        f_ref[...] = (h * lax.rsqrt(ms + EPS) * g_ref[...]).astype(BF16)
        acc_ref[...] = jnp.zeros_like(acc_ref)

    f = f_ref[...]
    gate = _dot(f, wg_ref[...])
    up = _dot(f, wu_ref[...])
    acc_ref[...] += _dot((_silu(gate) * up).astype(BF16), wd_ref[...])

    @pl.when(j == pl.num_programs(1) - 1)
    def _():
        o_ref[...] = h_ref[...] + acc_ref[...]


def _ffn(h2d, g_ffn, w_gate, w_up, w_down, tm, tf):
    m = h2d.shape[0]
    return pl.pallas_call(
        _ffn_kernel,
        grid=(m // tm, D_FF // tf),
        in_specs=[
            pl.BlockSpec((tm, D_MODEL), lambda i, j: (i, 0)),
            pl.BlockSpec((1, D_MODEL), lambda i, j: (0, 0)),
            pl.BlockSpec((D_MODEL, tf), lambda i, j: (0, j)),
            pl.BlockSpec((D_MODEL, tf), lambda i, j: (0, j)),
            pl.BlockSpec((tf, D_MODEL), lambda i, j: (j, 0)),
        ],
        out_specs=pl.BlockSpec((tm, D_MODEL), lambda i, j: (i, 0)),
        out_shape=jax.ShapeDtypeStruct((m, D_MODEL), F32),
        scratch_shapes=[pltpu.VMEM((tm, D_MODEL), BF16), pltpu.VMEM((tm, D_MODEL), F32)],
        compiler_params=_cparams("parallel", "arbitrary"),
        name="ffn",
    )(h2d, g_ffn, w_gate, w_up, w_down)


_EXPAND = np.kron(np.eye(SSD_HEADS), np.ones((1, SSD_HEAD_DIM)))


def _prepare(meta_tokens, g_mix, w_in, q_norm, k_norm, rpb, conv_w, conv_b, dt_bias_f, dt_bias_b,
             a_log_f, a_log_b, d_skip, ssd_norm, w_out, g_ffn, w_gate, w_up, w_down):
    l = 0
    p = dict(
        g_mix=g_mix[l].reshape(1, D_MODEL),
        w_in=w_in[l].astype(BF16),
        q_gain=jnp.tile(q_norm[l].astype(F32) * HEAD_DIM ** -0.5, IN_TN // HEAD_DIM).reshape(1, IN_TN),
        k_gain=jnp.tile(k_norm[l].astype(F32), N_KV_HEADS).reshape(1, KV_W),
        ones_bd=jnp.asarray(np.kron(np.eye(256 // HEAD_DIM), np.ones((HEAD_DIM, HEAD_DIM))), BF16),
        bias_tab=_bias_table(rpb[l]),
        conv_w=conv_w[l].astype(F32),
        conv_b=conv_b[l].astype(F32).reshape(1, CONV_DIM),
        dt_bias=jnp.concatenate([dt_bias_f[l], dt_bias_b[l]]).astype(F32).reshape(1, 2 * SSD_HEADS),
        a_log=jnp.concatenate([a_log_f[l], a_log_b[l]]).astype(F32).reshape(1, 2 * SSD_HEADS),
        d_skip=jnp.repeat(d_skip[l].astype(F32), SSD_HEAD_DIM).reshape(1, SSD_INNER),
        ssd_norm=ssd_norm[l].astype(F32).reshape(1, SSD_INNER),
        tril=jnp.asarray(np.tril(np.ones((CHUNK, CHUNK))), BF16),
        triu=jnp.asarray(np.triu(np.ones((CHUNK, CHUNK))), BF16),
        expand_f=jnp.asarray(np.concatenate([_EXPAND, 0 * _EXPAND]), BF16),
        expand_b=jnp.asarray(np.concatenate([0 * _EXPAND, _EXPAND]), BF16),
        w_out_a=w_out[l, :ATTN_W].astype(BF16),
        w_out_s=w_out[l, ATTN_W:].astype(BF16),
        g_ffn=g_ffn[l].reshape(1, D_MODEL),
        w_gate=w_gate[l].astype(BF16),
        w_up=w_up[l].astype(BF16),
        w_down=w_down[l].astype(BF16),
    )
    proj_m, dt_m = _in_proj(meta_tokens.astype(F32), p["g_mix"], p["w_in"], p["q_gain"],
                            p["k_gain"], p["ones_bd"], tm=N_META)
    p["k_meta"] = proj_m[:, COL_K:COL_K + KV_W]
    p["v_meta"] = proj_m[:, COL_V:COL_V + KV_W]
    p["proj_meta"] = proj_m
    p["dt_meta"] = dt_m
    return p


def _trunk(x, p):
    b, s, _ = x.shape
    m = b * s
    x2d = x.reshape(m, D_MODEL)
    proj, dt_raw = _in_proj(x2d, p["g_mix"], p["w_in"], p["q_gain"], p["k_gain"], p["ones_bd"], tm=512)
    proj3 = proj.reshape(b, s, PROJ_W)
    dt3 = dt_raw.reshape(b, s, 2 * SSD_HEADS)

    attn = _attention(proj3, p["bias_tab"], p["k_meta"], p["v_meta"])

    xbc_meta = p["proj_meta"][:, COL_XBC:COL_XBC + CONV_DIM]
    zeros_h = jnp.zeros((b, HALO, CONV_DIM), BF16)
    xc = _conv(proj3, COL_XBC, jnp.broadcast_to(xbc_meta[None], (b, HALO, CONV_DIM)), zeros_h,
               p["conv_w"], p["conv_b"], tc=512, cw=1024, shared=False)
    meta_in = jnp.concatenate([jnp.broadcast_to(xbc_meta[None], (b, N_META, CONV_DIM)),
                               proj3[:, :CHUNK - N_META, COL_XBC:COL_XBC + CONV_DIM]], axis=1)
    xc_meta = _conv(meta_in, 0, zeros_h, zeros_h, p["conv_w"], p["conv_b"], tc=CHUNK, cw=1024, shared=False)

    dt_meta = jnp.broadcast_to(jnp.pad(p["dt_meta"], ((0, CHUNK - N_META), (0, 0)))[None],
                               (b, CHUNK, 2 * SSD_HEADS))
    zero_state = jnp.zeros((b, SSD_GROUPS, SSD_STATE, GROUP_W), F32)
    _, state0 = _ssd_forward(xc_meta, dt_meta, p["dt_bias"], p["a_log"], p["tril"], p["expand_f"],
                             zero_state, n_valid=N_META)
    y_f, _ = _ssd_forward(xc, dt3, p["dt_bias"], p["a_log"], p["tril"], p["expand_f"], state0)
    ssd = _ssd_backward(xc, dt3, p["dt_bias"], p["a_log"], p["triu"], p["expand_b"], y_f, proj3,
                        p["d_skip"], p["ssd_norm"])

    h1 = _out_proj(attn.reshape(m, ATTN_W), ssd.reshape(m, SSD_INNER), p["w_out_a"], p["w_out_s"], x2d,
                   tm=512, tn=1024)
    out = _ffn(h1, p["g_ffn"], p["w_gate"], p["w_up"], p["w_down"], tm=512, tf=512)
    return out.reshape(b, s, D_MODEL)


def kernel(x_prompt, x_sample, meta_tokens, g_mix, w_in, q_norm, k_norm, rpb, conv_w, conv_b, dt_bias_f, dt_bias_b, a_log_f, a_log_b, d_skip, ssd_norm, w_out, g_ffn, w_gate, w_up, w_down):
    p = _prepare(meta_tokens, g_mix, w_in, q_norm, k_norm, rpb, conv_w, conv_b, dt_bias_f, dt_bias_b,
                 a_log_f, a_log_b, d_skip, ssd_norm, w_out, g_ffn, w_gate, w_up, w_down)
    return (_trunk(x_prompt, p), _trunk(x_sample, p))
```

```python
import functools

import numpy as np
import jax
import jax.numpy as jnp
from jax import lax
from jax.experimental import pallas as pl
from jax.experimental.pallas import tpu as pltpu

F32 = jnp.float32
BF16 = jnp.bfloat16

D_MODEL = 2048
N_META = 16
GRID_W = 64
WIN_ROWS = 8
WIN_COLS = 16
N_Q_HEADS = 32
N_KV_HEADS = 8
Q_PER_KV = N_Q_HEADS // N_KV_HEADS
HEAD_DIM = 64
ATTN_W = N_Q_HEADS * HEAD_DIM
KV_W = N_KV_HEADS * HEAD_DIM
SSD_INNER = 2 * D_MODEL
SSD_HEAD_DIM = 64
SSD_HEADS = SSD_INNER // SSD_HEAD_DIM
SSD_GROUPS = 8
SSD_HEADS_PER_GROUP = SSD_HEADS // SSD_GROUPS
SSD_STATE = 128
D_CONV = 4
CHUNK = 128
CONV_DIM = SSD_INNER + 2 * SSD_GROUPS * SSD_STATE
D_FF = 5632
EPS = 1e-6

GROUP_W = SSD_HEADS_PER_GROUP * SSD_HEAD_DIM
PROJ_W = ATTN_W + 2 * KV_W + SSD_INNER + CONV_DIM
COL_Z = 0
COL_XBC = SSD_INNER
COL_Q = COL_XBC + CONV_DIM
COL_K = COL_Q + ATTN_W
COL_V = COL_K + KV_W
NEG = -1e30
VMEM_LIMIT = 56 * 1024 * 1024


def _cparams(*sem):
    return pltpu.CompilerParams(dimension_semantics=sem, vmem_limit_bytes=VMEM_LIMIT)


def _dot(a, b):
    return jnp.dot(a, b, preferred_element_type=F32)


def _dot_nt(a, b):
    return lax.dot_general(a, b, (((1,), (1,)), ((), ())), preferred_element_type=F32)


def _dot_tn(a, b):
    return lax.dot_general(a, b, (((0,), (0,)), ((), ())), preferred_element_type=F32)


def _split2(x):
    hi = x.astype(BF16)
    lo = (x - hi.astype(F32)).astype(BF16)
    return hi, lo


def _split3(x):
    hi = x.astype(BF16)
    r = x - hi.astype(F32)
    mid = r.astype(BF16)
    lo = (r - mid.astype(F32)).astype(BF16)
    return hi, mid, lo


def _silu(x):
    return x / (1.0 + jnp.exp(-x))


IN_TN = 1024
N_QK_TILES = ATTN_W // IN_TN
KV_TILE = N_QK_TILES
Z_LAST_TILE = KV_TILE + SSD_INNER // IN_TN


MXU_W = 256


def _head_norm(a, gain, ones_bd):
    hi, lo = _split2(a * a)
    ms = (_dot(hi, ones_bd) + _dot(lo, ones_bd)) * (1.0 / HEAD_DIM)
    return a * lax.rsqrt(ms + EPS) * gain


def _inproj_kernel(x_ref, g_ref, w_ref, wdt_ref, qg_ref, kg_ref, ones_ref, o_ref, dt_ref, u_ref):
    j = pl.program_id(1)

    @pl.when(j == 0)
    def _():
        x = x_ref[...]
        ms = jnp.mean(x * x, axis=-1, keepdims=True)
        u = (x * lax.rsqrt(ms + EPS) * g_ref[...]).astype(BF16)
        u_ref[...] = u
        dt_ref[...] = _dot(u, wdt_ref[...])

    def emit(finish, split):
        u = u_ref[...]
        acc = None if split else _dot(u, w_ref[...])
        for c in range(IN_TN // MXU_W):
            cs = slice(c * MXU_W, (c + 1) * MXU_W)
            a = _dot(u, w_ref[:, cs]) if split else acc[:, cs]
            o_ref[:, cs] = finish(c, cs, a).astype(BF16)

    @pl.when(j < N_QK_TILES)
    def _():
        emit(lambda c, cs, a: _head_norm(a, qg_ref[:, cs], ones_ref[...]), split=False)

    @pl.when(j == KV_TILE)
    def _():
        emit(lambda c, cs, a: _head_norm(a, kg_ref[:, cs], ones_ref[...]) if c < KV_W // MXU_W else a,
             split=False)

    @pl.when((j > KV_TILE) & (j <= Z_LAST_TILE))
    def _():
        emit(lambda c, cs, a: _silu(a), split=True)

    @pl.when(j > Z_LAST_TILE)
    def _():
        emit(lambda c, cs, a: a, split=True)


def _in_proj(x2d, g_mix, w_in, q_gain, k_gain, ones_bd, tm):
    m = x2d.shape[0]
    n_tiles = PROJ_W // IN_TN
    dst = lambda j: jnp.where(j <= KV_TILE, j + (COL_Q // IN_TN), j - (KV_TILE + 1))
    return pl.pallas_call(
        _inproj_kernel,
        grid=(m // tm, n_tiles),
        in_specs=[
            pl.BlockSpec((tm, D_MODEL), lambda i, j: (i, 0)),
            pl.BlockSpec((1, D_MODEL), lambda i, j: (0, 0)),
            pl.BlockSpec((D_MODEL, IN_TN), lambda i, j: (0, j)),
            pl.BlockSpec((D_MODEL, 2 * SSD_HEADS), lambda i, j: (0, PROJ_W // (2 * SSD_HEADS))),
            pl.BlockSpec((1, IN_TN), lambda i, j: (0, 0)),
            pl.BlockSpec((1, KV_W), lambda i, j: (0, 0)),
            pl.BlockSpec((256, 256), lambda i, j: (0, 0)),
        ],
        out_specs=[
            pl.BlockSpec((tm, IN_TN), lambda i, j: (i, dst(j))),
            pl.BlockSpec((tm, 2 * SSD_HEADS), lambda i, j: (i, 0)),
        ],
        out_shape=[
            jax.ShapeDtypeStruct((m, PROJ_W), BF16),
            jax.ShapeDtypeStruct((m, 2 * SSD_HEADS), F32),
        ],
        scratch_shapes=[pltpu.VMEM((tm, D_MODEL), BF16)],
        compiler_params=_cparams("parallel", "arbitrary"),
        name="in_proj",
    )(x2d, g_mix, w_in, w_in, q_gain, k_gain, ones_bd)


N_BAND = WIN_ROWS * GRID_W
HALF_W = GRID_W // 2
META_AT = {0: GRID_W - N_META, 1: 0}


def _swap_halves(t):
    return jnp.concatenate([t[:, HEAD_DIM:], t[:, :HEAD_DIM]], axis=1)


def _with_meta(band, meta, hf):
    at = META_AT[hf]
    parts = [band[:at]] if at else []
    return jnp.concatenate(parts + [meta, band[at + N_META:]], axis=0)


def _attn_kernel(q_ref, k_ref, v_ref, bias_ref, km_ref, vm_ref, o_ref, *, rows):
    r = pl.program_id(1)
    rs = jnp.clip(r - WIN_ROWS // 2, 0, rows - WIN_ROWS)
    start = pl.multiple_of(rs * GRID_W, GRID_W)
    lo = lax.broadcasted_iota(jnp.int32, (HALF_W, 2 * HEAD_DIM), 1) < HEAD_DIM
    hi = jnp.logical_not(lo)
    ones = jnp.ones((N_BAND, 2 * HEAD_DIM), BF16)
    units = [(kp, hf) for kp in range(N_KV_HEADS // 2) for hf in range(2)]

    def scores(kp, hf):
        ps = slice(kp * 2 * HEAD_DIM, (kp + 1) * 2 * HEAD_DIM)
        kb = _with_meta(k_ref[0, pl.ds(start, N_BAND), ps], km_ref[:, ps], hf)
        tiles = []
        for par in range(2):
            keep = lo if par == 0 else hi
            for t in range(Q_PER_KV // 2):
                c0 = ((2 * kp + par) * 2 + t) * 128
                qt = q_ref[0, hf * HALF_W:(hf + 1) * HALF_W, c0:c0 + 128]
                qs = _swap_halves(qt)
                for piece in ((qt, qs) if par == 0 else (qs, qt)):
                    tiles.append(jnp.where(keep, piece, jnp.zeros_like(piece)))
        return _dot_nt(jnp.concatenate(tiles, axis=0), kb)

    d0 = WIN_ROWS - 1 - (r - rs)
    lane = lax.broadcasted_iota(jnp.int32, (1, 2 * HEAD_DIM), 1)
    s_next = scores(*units[0])
    for idx, (kp, hf) in enumerate(units):
        s_raw = s_next
        if idx + 1 < len(units):
            s_next = scores(*units[idx + 1])
        ps = slice(kp * 2 * HEAD_DIM, (kp + 1) * 2 * HEAD_DIM)
        vb = _with_meta(v_ref[0, pl.ds(start, N_BAND), ps], vm_ref[:, ps], hf)
        s = []
        for j in range(WIN_ROWS // 2):
            bias = bias_ref[kp, hf, d0 + 2 * j]
            if j == 0:
                at = META_AT[hf]
                bias = jnp.where((lane >= at) & (lane < at + N_META), 0.0, bias)
            s.append(s_raw[:, j * 128:(j + 1) * 128] + bias)
        m = jnp.max(jnp.maximum(jnp.maximum(s[0], s[1]), jnp.maximum(s[2], s[3])), axis=1, keepdims=True)
        p = jnp.concatenate([jnp.exp(sj - m) for sj in s], axis=1).astype(BF16)
        oa = _dot(p, jnp.concatenate([vb, ones], axis=1))
        o = oa[:, :128] / oa[:, 128:]
        for par in range(2):
            for t in range(Q_PER_KV // 2):
                base = (par * Q_PER_KV + 2 * t) * HALF_W
                a = o[base:base + HALF_W]
                b = o[base + HALF_W:base + 2 * HALF_W]
                if par == 0:
                    tile = jnp.where(lo, a, pltpu.roll(b, HEAD_DIM, 1))
                else:
                    tile = jnp.where(lo, pltpu.roll(a, HEAD_DIM, 1), b)
                c0 = ((2 * kp + par) * 2 + t) * 128
                o_ref[0, hf * HALF_W:(hf + 1) * HALF_W, c0:c0 + 128] = tile.astype(BF16)


def _attention(proj3, bias_tab, k_meta, v_meta):
    b, s, _ = proj3.shape
    rows = s // GRID_W
    return pl.pallas_call(
        functools.partial(_attn_kernel, rows=rows),
        grid=(b, rows),
        in_specs=[
            pl.BlockSpec((1, GRID_W, ATTN_W), lambda i, r: (i, r, COL_Q // ATTN_W)),
            pl.BlockSpec((1, s, KV_W), lambda i, r: (i, 0, COL_K // KV_W)),
            pl.BlockSpec((1, s, KV_W), lambda i, r: (i, 0, COL_V // KV_W)),
            pl.BlockSpec(bias_tab.shape, lambda i, r: (0, 0, 0, 0, 0), pipeline_mode=pl.Buffered(1)),
            pl.BlockSpec((N_META, KV_W), lambda i, r: (0, 0)),
            pl.BlockSpec((N_META, KV_W), lambda i, r: (0, 0)),
        ],
        out_specs=pl.BlockSpec((1, GRID_W, ATTN_W), lambda i, r: (i, r, 0)),
        out_shape=jax.ShapeDtypeStruct((b, s, ATTN_W), BF16),
        compiler_params=_cparams("parallel", "arbitrary"),
        name="na_attention",
    )(proj3, proj3, proj3, bias_tab, k_meta, v_meta)


def _bias_table(rpb):
    c = np.arange(GRID_W)
    cs = np.clip(c - WIN_COLS // 2, 0, GRID_W - WIN_COLS)
    inwin = (c[None, :] >= cs[:, None]) & (c[None, :] < cs[:, None] + WIN_COLS)
    dc = np.clip(c[None, :] - c[:, None] + WIN_COLS - 1, 0, 2 * WIN_COLS - 2)
    slab = jnp.where(inwin[None, None], rpb.astype(F32)[:, :, dc], NEG)
    nd = 2 * WIN_ROWS - 1
    slab = slab.reshape(N_KV_HEADS // 2, 2 * Q_PER_KV, nd, 2, HALF_W, GRID_W)
    slab = slab.transpose(0, 3, 2, 1, 4, 5).reshape(N_KV_HEADS // 2, 2, nd, 2 * Q_PER_KV * HALF_W, GRID_W)
    return jnp.concatenate([slab[:, :, :nd - 1], slab[:, :, 1:]], axis=-1)


HALO = 16
SUBLANES = 8


def _conv_kernel(cur_ref, prev_ref, next_ref, lh_ref, rh_ref, w_ref, b_ref, o_ref):
    i = pl.program_id(1)
    last = pl.num_programs(1) - 1
    tc = cur_ref.shape[1]
    before = jnp.where(i == 0, lh_ref[0], prev_ref[0]).astype(F32)
    after = jnp.where(i == last, rh_ref[0], next_ref[0]).astype(F32)
    cw = cur_ref.shape[2]
    ng = tc // SUBLANES
    x = cur_ref[0].astype(F32).reshape(ng, SUBLANES, cw)
    head = before[HALO - SUBLANES:].reshape(1, SUBLANES, cw)
    tail = after[:SUBLANES].reshape(1, SUBLANES, cw)
    sub = lax.broadcasted_iota(jnp.int32, (1, SUBLANES, cw), 1)
    x_prev = jnp.concatenate([head, x[:-1]], axis=0)
    x_next = jnp.concatenate([x[1:], tail], axis=0)

    def delayed(d):
        return pltpu.roll(jnp.where(sub >= SUBLANES - d, x_prev, x), d, 1)

    ahead = pltpu.roll(jnp.where(sub == 0, x_next, x), SUBLANES - 1, 1)
    w = w_ref[...].reshape(D_CONV, 1, cw)
    y = (b_ref[...].reshape(1, 1, cw) + w[0:1] * delayed(2) + w[1:2] * delayed(1) + w[2:3] * x
         + w[3:4] * ahead)
    o_ref[0] = _silu(y).reshape(tc, cw).astype(BF16)


def _conv(src, col0, lh, rh, conv_w, conv_b, tc, cw, shared):
    b = lh.shape[0]
    s = src.shape[1]
    hb = tc // HALO
    nhb = s // HALO
    c0 = col0 // cw
    bi = (lambda i: 0) if shared else (lambda i: i)
    return pl.pallas_call(
        _conv_kernel,
        grid=(b, s // tc, CONV_DIM // cw),
        in_specs=[
            pl.BlockSpec((1, tc, cw), lambda i, t, c: (bi(i), t, c0 + c)),
            pl.BlockSpec((1, HALO, cw), lambda i, t, c: (bi(i), jnp.maximum(t * hb - 1, 0), c0 + c)),
            pl.BlockSpec((1, HALO, cw), lambda i, t, c: (bi(i), jnp.minimum((t + 1) * hb, nhb - 1), c0 + c)),
            pl.BlockSpec((1, HALO, cw), lambda i, t, c: (i, 0, c)),
            pl.BlockSpec((1, HALO, cw), lambda i, t, c: (i, 0, c)),
            pl.BlockSpec((D_CONV, cw), lambda i, t, c: (0, c)),
            pl.BlockSpec((1, cw), lambda i, t, c: (0, c)),
        ],
        out_specs=pl.BlockSpec((1, tc, cw), lambda i, t, c: (i, t, c)),
        out_shape=jax.ShapeDtypeStruct((b, s, CONV_DIM), BF16),
        compiler_params=_cparams("parallel", "arbitrary", "arbitrary"),
        name="conv_silu",
    )(src, src, src, lh, rh, conv_w, conv_b)


def _softplus(x):
    return jnp.maximum(x, 0.0) + jnp.log1p(jnp.exp(-jnp.abs(x)))


def _ssd_chunk(x_ref, b_ref, c_ref, dtraw_ref, dtb_ref, alog_ref, tri_ref, exp_ref, state_ref,
               *, reverse, n_valid):
    L = x_ref.shape[1]
    off = SSD_HEADS if reverse else 0
    dt = _softplus(dtraw_ref[0] + dtb_ref[...])
    if n_valid < L:
        dt = jnp.where(lax.broadcasted_iota(jnp.int32, dt.shape, 0) < n_valid, dt, 0.0)
    da = dt * (-jnp.exp(alog_ref[...]))
    tri = tri_ref[...]
    acs = sum(_dot(tri, piece) for piece in _split3(da))
    tot = acs[0:1] if reverse else acs[L - 1:L]
    src_t = (acs - jnp.log(dt)).T
    expand = exp_ref[...]

    def widen(v):
        return _dot(v.astype(BF16), expand)

    e_out = widen(jnp.exp(acs))
    e_in = widen(jnp.exp(tot - acs) * dt)
    e_tot = e_out[0:1] if reverse else e_out[L - 1:L]
    xw = (x_ref[0].astype(F32) * e_in).astype(BF16)

    li = lax.broadcasted_iota(jnp.int32, (L, L), 0)
    si = lax.broadcasted_iota(jnp.int32, (L, L), 1)
    keep = (si >= li) if reverse else (si <= li)
    lane = lax.broadcasted_iota(jnp.int32, (L, 2 * SSD_HEAD_DIM), 1)
    left = lane < SSD_HEAD_DIM

    ys = []
    for g in range(SSD_GROUPS):
        gs = slice(g * GROUP_W, (g + 1) * GROUP_W)
        ns = slice(g * SSD_STATE, (g + 1) * SSD_STATE)
        bg = b_ref[0, :, ns]
        cg = c_ref[0, :, ns]
        cb = _dot_nt(cg, bg)
        st = state_ref[g]
        y_inter = _dot(cg, st.astype(BF16)) * e_out[:, gs]
        state_ref[g] = st * e_tot[:, gs] + _dot_tn(bg, xw[:, gs])
        pairs = []
        for j in range(SSD_HEADS_PER_GROUP // 2):
            ms = []
            for h in (off + g * SSD_HEADS_PER_GROUP + 2 * j, off + g * SSD_HEADS_PER_GROUP + 2 * j + 1):
                seg = acs[:, h:h + 1] - src_t[h:h + 1, :]
                ms.append((cb * jnp.exp(jnp.where(keep, seg, -jnp.inf))).astype(BF16))
            xp = x_ref[0, :, g * GROUP_W + j * 128:g * GROUP_W + (j + 1) * 128]
            zero = jnp.zeros_like(xp)
            xbd = jnp.concatenate([jnp.where(left, xp, zero), jnp.where(left, zero, xp)], axis=0)
            pairs.append(_dot(jnp.concatenate(ms, axis=1), xbd))
        ys.append(jnp.concatenate(pairs, axis=1) + y_inter)
    return jnp.concatenate(ys, axis=1)


def _ssd_fwd_kernel(x_ref, b_ref, c_ref, dtraw_ref, dtb_ref, alog_ref, tri_ref, exp_ref, init_ref,
                    y_ref, fin_ref, state_ref, *, n_valid):
    c = pl.program_id(1)

    @pl.when(c == 0)
    def _():
        state_ref[...] = init_ref[0]

    y = _ssd_chunk(x_ref, b_ref, c_ref, dtraw_ref, dtb_ref, alog_ref, tri_ref, exp_ref, state_ref,
                   reverse=False, n_valid=n_valid)
    y_ref[0] = y

    @pl.when(c == pl.num_programs(1) - 1)
    def _():
        fin_ref[0] = state_ref[...]


def _ssd_bwd_kernel(x_ref, b_ref, c_ref, dtraw_ref, dtb_ref, alog_ref, tri_ref, exp_ref,
                    yf_ref, z_ref, dskip_ref, nw_ref, o_ref, state_ref):
    c = pl.program_id(1)

    @pl.when(c == 0)
    def _():
        state_ref[...] = jnp.zeros_like(state_ref)

    L = x_ref.shape[1]
    y = _ssd_chunk(x_ref, b_ref, c_ref, dtraw_ref, dtb_ref, alog_ref, tri_ref, exp_ref, state_ref,
                   reverse=True, n_valid=L)
    y = y + yf_ref[0] + x_ref[0].astype(F32) * dskip_ref[...]
    y = y * z_ref[0].astype(F32)
    outs = []
    for g in range(SSD_GROUPS):
        yg = y[:, g * GROUP_W:(g + 1) * GROUP_W]
        ms = jnp.mean(yg * yg, axis=-1, keepdims=True)
        outs.append(yg * lax.rsqrt(ms + EPS))
    o_ref[0] = (jnp.concatenate(outs, axis=1) * nw_ref[...]).astype(BF16)


def _ssd_common_specs(L, cmap):
    nb = SSD_GROUPS * SSD_STATE
    return [
        pl.BlockSpec((1, L, SSD_INNER), lambda i, c: (i, cmap(c), 0)),
        pl.BlockSpec((1, L, nb), lambda i, c: (i, cmap(c), SSD_INNER // nb)),
        pl.BlockSpec((1, L, nb), lambda i, c: (i, cmap(c), SSD_INNER // nb + 1)),
    ]


def _const_spec(shape):
    nd = len(shape)
    return pl.BlockSpec(shape, lambda i, c: (0,) * nd)


def _ssd_forward(xc, dt_raw, dt_bias, a_log, tri, expand, init, n_valid=CHUNK):
    b, s, _ = xc.shape
    L = CHUNK
    st_shape = (SSD_GROUPS, SSD_STATE, GROUP_W)
    return pl.pallas_call(
        functools.partial(_ssd_fwd_kernel, n_valid=n_valid),
        grid=(b, s // L),
        in_specs=_ssd_common_specs(L, lambda c: c) + [
            pl.BlockSpec((1, L, 2 * SSD_HEADS), lambda i, c: (i, c, 0)),
            _const_spec((1, 2 * SSD_HEADS)), _const_spec((1, 2 * SSD_HEADS)),
            _const_spec((L, L)), _const_spec((2 * SSD_HEADS, SSD_INNER)),
            pl.BlockSpec((1,) + st_shape, lambda i, c: (i, 0, 0, 0)),
        ],
        out_specs=[
            pl.BlockSpec((1, L, SSD_INNER), lambda i, c: (i, c, 0)),
            pl.BlockSpec((1,) + st_shape, lambda i, c: (i, 0, 0, 0)),
        ],
        out_shape=[
            jax.ShapeDtypeStruct((b, s, SSD_INNER), F32),
            jax.ShapeDtypeStruct((b,) + st_shape, F32),
        ],
        scratch_shapes=[pltpu.VMEM(st_shape, F32)],
        compiler_params=_cparams("parallel", "arbitrary"),
        name="ssd_forward",
    )(xc, xc, xc, dt_raw, dt_bias, a_log, tri, expand, init)


def _ssd_backward(xc, dt_raw, dt_bias, a_log, tri, expand, y_f, proj3, d_skip, norm_w):
    b, s, _ = xc.shape
    L = CHUNK
    nc = s // L
    rev = lambda c: nc - 1 - c
    return pl.pallas_call(
        _ssd_bwd_kernel,
        grid=(b, nc),
        in_specs=_ssd_common_specs(L, rev) + [
            pl.BlockSpec((1, L, 2 * SSD_HEADS), lambda i, c: (i, rev(c), 0)),
            _const_spec((1, 2 * SSD_HEADS)), _const_spec((1, 2 * SSD_HEADS)),
            _const_spec((L, L)), _const_spec((2 * SSD_HEADS, SSD_INNER)),
            pl.BlockSpec((1, L, SSD_INNER), lambda i, c: (i, rev(c), 0)),
            pl.BlockSpec((1, L, SSD_INNER), lambda i, c: (i, rev(c), COL_Z // SSD_INNER)),
            _const_spec((1, SSD_INNER)), _const_spec((1, SSD_INNER)),
        ],
        out_specs=pl.BlockSpec((1, L, SSD_INNER), lambda i, c: (i, rev(c), 0)),
        out_shape=jax.ShapeDtypeStruct((b, s, SSD_INNER), BF16),
        scratch_shapes=[pltpu.VMEM((SSD_GROUPS, SSD_STATE, GROUP_W), F32)],
        compiler_params=_cparams("parallel", "arbitrary"),
        name="ssd_backward",
    )(xc, xc, xc, dt_raw, dt_bias, a_log, tri, expand, y_f, proj3, d_skip, norm_w)


def _outproj_kernel(a_ref, s_ref, wa_ref, ws_ref, x_ref, o_ref):
    o_ref[...] = x_ref[...] + _dot(a_ref[...], wa_ref[...]) + _dot(s_ref[...], ws_ref[...])


def _out_proj(attn2, ssd2, w_a, w_s, x2d, tm, tn):
    m = x2d.shape[0]
    return pl.pallas_call(
        _outproj_kernel,
        grid=(D_MODEL // tn, m // tm),
        in_specs=[
            pl.BlockSpec((tm, ATTN_W), lambda j, i: (i, 0)),
            pl.BlockSpec((tm, SSD_INNER), lambda j, i: (i, 0)),
            pl.BlockSpec((ATTN_W, tn), lambda j, i: (0, j)),
            pl.BlockSpec((SSD_INNER, tn), lambda j, i: (0, j)),
            pl.BlockSpec((tm, tn), lambda j, i: (i, j)),
        ],
        out_specs=pl.BlockSpec((tm, tn), lambda j, i: (i, j)),
        out_shape=jax.ShapeDtypeStruct((m, D_MODEL), F32),
        compiler_params=_cparams("parallel", "parallel"),
        name="out_proj",
    )(attn2, ssd2, w_a, w_s, x2d)


def _ffn_kernel(h_ref, g_ref, wg_ref, wu_ref, wd_ref, o_ref, f_ref, acc_ref):
    j = pl.program_id(1)

    @pl.when(j == 0)
    def _():
        h = h_ref[...]
        ms = jnp.mean(h * h, axis=-1, keepdims=True)
        f_ref[...] = (h * lax.rsqrt(ms + EPS) * g_ref[...]).astype(BF16)
        acc_ref[...] = jnp.zeros_like(acc_ref)

    f = f_ref[...]
    gate = _dot(f, wg_ref[...])
    up = _dot(f, wu_ref[...])
    acc_ref[...] += _dot((_silu(gate) * up).astype(BF16), wd_ref[...])

    @pl.when(j == pl.num_programs(1) - 1)
    def _():
        o_ref[...] = h_ref[...] + acc_ref[...]


def _ffn(h2d, g_ffn, w_gate, w_up, w_down, tm, tf):
    m = h2d.shape[0]
    return pl.pallas_call(
        _ffn_kernel,
        grid=(m // tm, D_FF // tf),
        in_specs=[
            pl.BlockSpec((tm, D_MODEL), lambda i, j: (i, 0)),
            pl.BlockSpec((1, D_MODEL), lambda i, j: (0, 0)),
            pl.BlockSpec((D_MODEL, tf), lambda i, j: (0, j)),
            pl.BlockSpec((D_MODEL, tf), lambda i, j: (0, j)),
            pl.BlockSpec((tf, D_MODEL), lambda i, j: (j, 0)),
        ],
        out_specs=pl.BlockSpec((tm, D_MODEL), lambda i, j: (i, 0)),
        out_shape=jax.ShapeDtypeStruct((m, D_MODEL), F32),
        scratch_shapes=[pltpu.VMEM((tm, D_MODEL), BF16), pltpu.VMEM((tm, D_MODEL), F32)],
        compiler_params=_cparams("parallel", "arbitrary"),
        name="ffn",
    )(h2d, g_ffn, w_gate, w_up, w_down)


_EXPAND = np.kron(np.eye(SSD_HEADS), np.ones((1, SSD_HEAD_DIM)))


def _prepare(meta_tokens, g_mix, w_in, q_norm, k_norm, rpb, conv_w, conv_b, dt_bias_f, dt_bias_b,
             a_log_f, a_log_b, d_skip, ssd_norm, w_out, g_ffn, w_gate, w_up, w_down):
    l = 0
    p = dict(
        g_mix=g_mix[l].reshape(1, D_MODEL),
        w_in=w_in[l].astype(BF16),
        q_gain=jnp.tile(q_norm[l].astype(F32) * HEAD_DIM ** -0.5, IN_TN // HEAD_DIM).reshape(1, IN_TN),
        k_gain=jnp.tile(k_norm[l].astype(F32), N_KV_HEADS).reshape(1, KV_W),
        ones_bd=jnp.asarray(np.kron(np.eye(256 // HEAD_DIM), np.ones((HEAD_DIM, HEAD_DIM))), BF16),
        bias_tab=_bias_table(rpb[l]),
        conv_w=conv_w[l].astype(F32),
        conv_b=conv_b[l].astype(F32).reshape(1, CONV_DIM),
        dt_bias=jnp.concatenate([dt_bias_f[l], dt_bias_b[l]]).astype(F32).reshape(1, 2 * SSD_HEADS),
        a_log=jnp.concatenate([a_log_f[l], a_log_b[l]]).astype(F32).reshape(1, 2 * SSD_HEADS),
        d_skip=jnp.repeat(d_skip[l].astype(F32), SSD_HEAD_DIM).reshape(1, SSD_INNER),
        ssd_norm=ssd_norm[l].astype(F32).reshape(1, SSD_INNER),
        tril=jnp.asarray(np.tril(np.ones((CHUNK, CHUNK))), BF16),
        triu=jnp.asarray(np.triu(np.ones((CHUNK, CHUNK))), BF16),
        expand_f=jnp.asarray(np.concatenate([_EXPAND, 0 * _EXPAND]), BF16),
        expand_b=jnp.asarray(np.concatenate([0 * _EXPAND, _EXPAND]), BF16),
        w_out_a=w_out[l, :ATTN_W].astype(BF16),
        w_out_s=w_out[l, ATTN_W:].astype(BF16),
        g_ffn=g_ffn[l].reshape(1, D_MODEL),
        w_gate=w_gate[l].astype(BF16),
        w_up=w_up[l].astype(BF16),
        w_down=w_down[l].astype(BF16),
    )
    proj_m, dt_m = _in_proj(meta_tokens.astype(F32), p["g_mix"], p["w_in"], p["q_gain"],
                            p["k_gain"], p["ones_bd"], tm=N_META)
    p["k_meta"] = proj_m[:, COL_K:COL_K + KV_W]
    p["v_meta"] = proj_m[:, COL_V:COL_V + KV_W]
    p["proj_meta"] = proj_m
    p["dt_meta"] = dt_m
    return p


def _trunk(x, p):
    b, s, _ = x.shape
    m = b * s
    x2d = x.reshape(m, D_MODEL)
    proj, dt_raw = _in_proj(x2d, p["g_mix"], p["w_in"], p["q_gain"], p["k_gain"], p["ones_bd"], tm=1024)
    proj3 = proj.reshape(b, s, PROJ_W)
    dt3 = dt_raw.reshape(b, s, 2 * SSD_HEADS)

    attn = _attention(proj3, p["bias_tab"], p["k_meta"], p["v_meta"])

    xbc_meta = p["proj_meta"][:, COL_XBC:COL_XBC + CONV_DIM]
    zeros_h = jnp.zeros((b, HALO, CONV_DIM), BF16)
    xc = _conv(proj3, COL_XBC, jnp.broadcast_to(xbc_meta[None], (b, HALO, CONV_DIM)), zeros_h,
               p["conv_w"], p["conv_b"], tc=512, cw=1024, shared=False)
    meta_in = jnp.concatenate([jnp.broadcast_to(xbc_meta[None], (b, N_META, CONV_DIM)),
                               proj3[:, :CHUNK - N_META, COL_XBC:COL_XBC + CONV_DIM]], axis=1)
    xc_meta = _conv(meta_in, 0, zeros_h, zeros_h, p["conv_w"], p["conv_b"], tc=CHUNK, cw=1024, shared=False)

    dt_meta = jnp.broadcast_to(jnp.pad(p["dt_meta"], ((0, CHUNK - N_META), (0, 0)))[None],
                               (b, CHUNK, 2 * SSD_HEADS))
    zero_state = jnp.zeros((b, SSD_GROUPS, SSD_STATE, GROUP_W), F32)
    _, state0 = _ssd_forward(xc_meta, dt_meta, p["dt_bias"], p["a_log"], p["tril"], p["expand_f"],
                             zero_state, n_valid=N_META)
    y_f, _ = _ssd_forward(xc, dt3, p["dt_bias"], p["a_log"], p["tril"], p["expand_f"], state0)
    ssd = _ssd_backward(xc, dt3, p["dt_bias"], p["a_log"], p["triu"], p["expand_b"], y_f, proj3,
                        p["d_skip"], p["ssd_norm"])

    h1 = _out_proj(attn.reshape(m, ATTN_W), ssd.reshape(m, SSD_INNER), p["w_out_a"], p["w_out_s"], x2d,
                   tm=512, tn=1024)
    out = _ffn(h1, p["g_ffn"], p["w_gate"], p["w_up"], p["w_down"], tm=512, tf=512)
    return out.reshape(b, s, D_MODEL)


def kernel(x_prompt, x_sample, meta_tokens, g_mix, w_in, q_norm, k_norm, rpb, conv_w, conv_b, dt_bias_f, dt_bias_b, a_log_f, a_log_b, d_skip, ssd_norm, w_out, g_ffn, w_gate, w_up, w_down):
    p = _prepare(meta_tokens, g_mix, w_in, q_norm, k_norm, rpb, conv_w, conv_b, dt_bias_f, dt_bias_b,
                 a_log_f, a_log_b, d_skip, ssd_norm, w_out, g_ffn, w_gate, w_up, w_down)
    return (_trunk(x_prompt, p), _trunk(x_sample, p))
```

```python
import functools

import numpy as np
import jax
import jax.numpy as jnp
from jax import lax
from jax.experimental import pallas as pl
from jax.experimental.pallas import tpu as pltpu

F32 = jnp.float32
BF16 = jnp.bfloat16

D_MODEL = 2048
N_META = 16
GRID_W = 64
WIN_ROWS = 8
WIN_COLS = 16
N_Q_HEADS = 32
N_KV_HEADS = 8
Q_PER_KV = N_Q_HEADS // N_KV_HEADS
HEAD_DIM = 64
ATTN_W = N_Q_HEADS * HEAD_DIM
KV_W = N_KV_HEADS * HEAD_DIM
SSD_INNER = 2 * D_MODEL
SSD_HEAD_DIM = 64
SSD_HEADS = SSD_INNER // SSD_HEAD_DIM
SSD_GROUPS = 8
SSD_HEADS_PER_GROUP = SSD_HEADS // SSD_GROUPS
SSD_STATE = 128
D_CONV = 4
CHUNK = 128
CONV_DIM = SSD_INNER + 2 * SSD_GROUPS * SSD_STATE
D_FF = 5632
EPS = 1e-6

GROUP_W = SSD_HEADS_PER_GROUP * SSD_HEAD_DIM
PROJ_W = ATTN_W + 2 * KV_W + SSD_INNER + CONV_DIM
COL_Z = 0
COL_XBC = SSD_INNER
COL_Q = COL_XBC + CONV_DIM
COL_K = COL_Q + ATTN_W
COL_V = COL_K + KV_W
NEG = -1e30
VMEM_LIMIT = 56 * 1024 * 1024


def _cparams(*sem):
    return pltpu.CompilerParams(dimension_semantics=sem, vmem_limit_bytes=VMEM_LIMIT)


def _dot(a, b):
    return jnp.dot(a, b, preferred_element_type=F32)


def _dot_nt(a, b):
    return lax.dot_general(a, b, (((1,), (1,)), ((), ())), preferred_element_type=F32)


def _dot_tn(a, b):
    return lax.dot_general(a, b, (((0,), (0,)), ((), ())), preferred_element_type=F32)


def _split2(x):
    hi = x.astype(BF16)
    lo = (x - hi.astype(F32)).astype(BF16)
    return hi, lo


def _split3(x):
    hi = x.astype(BF16)
    r = x - hi.astype(F32)
    mid = r.astype(BF16)
    lo = (r - mid.astype(F32)).astype(BF16)
    return hi, mid, lo


def _silu(x):
    return x / (1.0 + jnp.exp(-x))


IN_TN = 1024
N_QK_TILES = ATTN_W // IN_TN
KV_TILE = N_QK_TILES
Z_LAST_TILE = KV_TILE + SSD_INNER // IN_TN


MXU_W = 256


def _head_norm(a, gain, ones_bd):
    hi, lo = _split2(a * a)
    ms = (_dot(hi, ones_bd) + _dot(lo, ones_bd)) * (1.0 / HEAD_DIM)
    return a * lax.rsqrt(ms + EPS) * gain


def _inproj_kernel(x_ref, g_ref, w_ref, wdt_ref, qg_ref, kg_ref, ones_ref, o_ref, dt_ref, u_ref):
    j = pl.program_id(1)

    @pl.when(j == 0)
    def _():
        x = x_ref[...]
        ms = jnp.mean(x * x, axis=-1, keepdims=True)
        u = (x * lax.rsqrt(ms + EPS) * g_ref[...]).astype(BF16)
        u_ref[...] = u
        dt_ref[...] = _dot(u, wdt_ref[...])

    def emit(finish, split):
        u = u_ref[...]
        acc = None if split else _dot(u, w_ref[...])
        for c in range(IN_TN // MXU_W):
            cs = slice(c * MXU_W, (c + 1) * MXU_W)
            a = _dot(u, w_ref[:, cs]) if split else acc[:, cs]
            o_ref[:, cs] = finish(c, cs, a).astype(BF16)

    @pl.when(j < N_QK_TILES)
    def _():
        emit(lambda c, cs, a: _head_norm(a, qg_ref[:, cs], ones_ref[...]), split=False)

    @pl.when(j == KV_TILE)
    def _():
        emit(lambda c, cs, a: _head_norm(a, kg_ref[:, cs], ones_ref[...]) if c < KV_W // MXU_W else a,
             split=False)

    @pl.when((j > KV_TILE) & (j <= Z_LAST_TILE))
    def _():
        emit(lambda c, cs, a: _silu(a), split=True)

    @pl.when(j > Z_LAST_TILE)
    def _():
        emit(lambda c, cs, a: a, split=True)


def _in_proj(x2d, g_mix, w_in, q_gain, k_gain, ones_bd, tm):
    m = x2d.shape[0]
    n_tiles = PROJ_W // IN_TN
    dst = lambda j: jnp.where(j <= KV_TILE, j + (COL_Q // IN_TN), j - (KV_TILE + 1))
    return pl.pallas_call(
        _inproj_kernel,
        grid=(m // tm, n_tiles),
        in_specs=[
            pl.BlockSpec((tm, D_MODEL), lambda i, j: (i, 0)),
            pl.BlockSpec((1, D_MODEL), lambda i, j: (0, 0)),
            pl.BlockSpec((D_MODEL, IN_TN), lambda i, j: (0, j)),
            pl.BlockSpec((D_MODEL, 2 * SSD_HEADS), lambda i, j: (0, PROJ_W // (2 * SSD_HEADS))),
            pl.BlockSpec((1, IN_TN), lambda i, j: (0, 0)),
            pl.BlockSpec((1, KV_W), lambda i, j: (0, 0)),
            pl.BlockSpec((256, 256), lambda i, j: (0, 0)),
        ],
        out_specs=[
            pl.BlockSpec((tm, IN_TN), lambda i, j: (i, dst(j))),
            pl.BlockSpec((tm, 2 * SSD_HEADS), lambda i, j: (i, 0)),
        ],
        out_shape=[
            jax.ShapeDtypeStruct((m, PROJ_W), BF16),
            jax.ShapeDtypeStruct((m, 2 * SSD_HEADS), F32),
        ],
        scratch_shapes=[pltpu.VMEM((tm, D_MODEL), BF16)],
        compiler_params=_cparams("parallel", "arbitrary"),
        name="in_proj",
    )(x2d, g_mix, w_in, w_in, q_gain, k_gain, ones_bd)


N_BAND = WIN_ROWS * GRID_W
HALF_W = GRID_W // 2
ATTN_ROWS = 4
META_AT = {0: GRID_W - N_META, 1: 0}


def _swap_halves(t):
    return jnp.concatenate([t[:, HEAD_DIM:], t[:, :HEAD_DIM]], axis=1)


def _with_meta(band, meta, hf):
    at = META_AT[hf]
    parts = [band[:at]] if at else []
    return jnp.concatenate(parts + [meta, band[at + N_META:]], axis=0)


def _attn_kernel(q_ref, k_ref, v_ref, bias_ref, km_ref, vm_ref, o_ref, *, rows):
    n_rows = q_ref.shape[1] // GRID_W
    r = [pl.program_id(1) * n_rows + ri for ri in range(n_rows)]
    rs = [jnp.clip(x - WIN_ROWS // 2, 0, rows - WIN_ROWS) for x in r]
    start = [pl.multiple_of(x * GRID_W, GRID_W) for x in rs]
    d0 = [WIN_ROWS - 1 - (x - y) for x, y in zip(r, rs)]
    lo = lax.broadcasted_iota(jnp.int32, (HALF_W, 2 * HEAD_DIM), 1) < HEAD_DIM
    hi = jnp.logical_not(lo)
    ones = jnp.ones((N_BAND, 2 * HEAD_DIM), BF16)
    units = [(ri, kp, hf) for ri in range(n_rows) for kp in range(N_KV_HEADS // 2) for hf in range(2)]

    def scores(ri, kp, hf):
        ps = slice(kp * 2 * HEAD_DIM, (kp + 1) * 2 * HEAD_DIM)
        kb = _with_meta(k_ref[0, pl.ds(start[ri], N_BAND), ps], km_ref[:, ps], hf)
        q0 = ri * GRID_W + hf * HALF_W
        tiles = []
        for par in range(2):
            keep = lo if par == 0 else hi
            for t in range(Q_PER_KV // 2):
                c0 = ((2 * kp + par) * 2 + t) * 128
                qt = q_ref[0, q0:q0 + HALF_W, c0:c0 + 128]
                qs = _swap_halves(qt)
                for piece in ((qt, qs) if par == 0 else (qs, qt)):
                    tiles.append(jnp.where(keep, piece, jnp.zeros_like(piece)))
        return _dot_nt(jnp.concatenate(tiles, axis=0), kb)

    lane = lax.broadcasted_iota(jnp.int32, (1, 2 * HEAD_DIM), 1)
    s_next = scores(*units[0])
    for idx, (ri, kp, hf) in enumerate(units):
        s_raw = s_next
        if idx + 1 < len(units):
            s_next = scores(*units[idx + 1])
        ps = slice(kp * 2 * HEAD_DIM, (kp + 1) * 2 * HEAD_DIM)
        vb = _with_meta(v_ref[0, pl.ds(start[ri], N_BAND), ps], vm_ref[:, ps], hf)
        q0 = ri * GRID_W + hf * HALF_W
        s = []
        for j in range(WIN_ROWS // 2):
            bias = bias_ref[kp, hf, d0[ri] + 2 * j]
            if j == 0:
                at = META_AT[hf]
                bias = jnp.where((lane >= at) & (lane < at + N_META), 0.0, bias)
            s.append(s_raw[:, j * 128:(j + 1) * 128] + bias)
        m = jnp.max(jnp.maximum(jnp.maximum(s[0], s[1]), jnp.maximum(s[2], s[3])), axis=1, keepdims=True)
        p = jnp.concatenate([jnp.exp(sj - m) for sj in s], axis=1).astype(BF16)
        oa = _dot(p, jnp.concatenate([vb, ones], axis=1))
        o = oa[:, :128] / oa[:, 128:]
        for par in range(2):
            for t in range(Q_PER_KV // 2):
                base = (par * Q_PER_KV + 2 * t) * HALF_W
                a = o[base:base + HALF_W]
                b = o[base + HALF_W:base + 2 * HALF_W]
                if par == 0:
                    tile = jnp.where(lo, a, pltpu.roll(b, HEAD_DIM, 1))
                else:
                    tile = jnp.where(lo, pltpu.roll(a, HEAD_DIM, 1), b)
                c0 = ((2 * kp + par) * 2 + t) * 128
                o_ref[0, q0:q0 + HALF_W, c0:c0 + 128] = tile.astype(BF16)


def _attention(proj3, bias_tab, k_meta, v_meta):
    b, s, _ = proj3.shape
    rows = s // GRID_W
    return pl.pallas_call(
        functools.partial(_attn_kernel, rows=rows),
        grid=(b, rows // ATTN_ROWS),
        in_specs=[
            pl.BlockSpec((1, ATTN_ROWS * GRID_W, ATTN_W), lambda i, r: (i, r, COL_Q // ATTN_W)),
            pl.BlockSpec((1, s, KV_W), lambda i, r: (i, 0, COL_K // KV_W)),
            pl.BlockSpec((1, s, KV_W), lambda i, r: (i, 0, COL_V // KV_W)),
            pl.BlockSpec(bias_tab.shape, lambda i, r: (0, 0, 0, 0, 0), pipeline_mode=pl.Buffered(1)),
            pl.BlockSpec((N_META, KV_W), lambda i, r: (0, 0)),
            pl.BlockSpec((N_META, KV_W), lambda i, r: (0, 0)),
        ],
        out_specs=pl.BlockSpec((1, ATTN_ROWS * GRID_W, ATTN_W), lambda i, r: (i, r, 0)),
        out_shape=jax.ShapeDtypeStruct((b, s, ATTN_W), BF16),
        compiler_params=_cparams("parallel", "arbitrary"),
        name="na_attention",
    )(proj3, proj3, proj3, bias_tab, k_meta, v_meta)


def _bias_table(rpb):
    c = np.arange(GRID_W)
    cs = np.clip(c - WIN_COLS // 2, 0, GRID_W - WIN_COLS)
    inwin = (c[None, :] >= cs[:, None]) & (c[None, :] < cs[:, None] + WIN_COLS)
    dc = np.clip(c[None, :] - c[:, None] + WIN_COLS - 1, 0, 2 * WIN_COLS - 2)
    slab = jnp.where(inwin[None, None], rpb.astype(F32)[:, :, dc], NEG)
    nd = 2 * WIN_ROWS - 1
    slab = slab.reshape(N_KV_HEADS // 2, 2 * Q_PER_KV, nd, 2, HALF_W, GRID_W)
    slab = slab.transpose(0, 3, 2, 1, 4, 5).reshape(N_KV_HEADS // 2, 2, nd, 2 * Q_PER_KV * HALF_W, GRID_W)
    return jnp.concatenate([slab[:, :, :nd - 1], slab[:, :, 1:]], axis=-1)


HALO = 16
SUBLANES = 8


def _conv_kernel(cur_ref, prev_ref, next_ref, lh_ref, rh_ref, w_ref, b_ref, o_ref):
    i = pl.program_id(1)
    last = pl.num_programs(1) - 1
    tc = cur_ref.shape[1]
    before = jnp.where(i == 0, lh_ref[0], prev_ref[0]).astype(F32)
    after = jnp.where(i == last, rh_ref[0], next_ref[0]).astype(F32)
    cw = cur_ref.shape[2]
    ng = tc // SUBLANES
    x = cur_ref[0].astype(F32).reshape(ng, SUBLANES, cw)
    head = before[HALO - SUBLANES:].reshape(1, SUBLANES, cw)
    tail = after[:SUBLANES].reshape(1, SUBLANES, cw)
    sub = lax.broadcasted_iota(jnp.int32, (1, SUBLANES, cw), 1)
    x_prev = jnp.concatenate([head, x[:-1]], axis=0)
    x_next = jnp.concatenate([x[1:], tail], axis=0)

    def delayed(d):
        return pltpu.roll(jnp.where(sub >= SUBLANES - d, x_prev, x), d, 1)

    ahead = pltpu.roll(jnp.where(sub == 0, x_next, x), SUBLANES - 1, 1)
    w = w_ref[...].reshape(D_CONV, 1, cw)
    y = (b_ref[...].reshape(1, 1, cw) + w[0:1] * delayed(2) + w[1:2] * delayed(1) + w[2:3] * x
         + w[3:4] * ahead)
    o_ref[0] = _silu(y).reshape(tc, cw).astype(BF16)


def _conv(src, col0, lh, rh, conv_w, conv_b, tc, cw, shared):
    b = lh.shape[0]
    s = src.shape[1]
    hb = tc // HALO
    nhb = s // HALO
    c0 = col0 // cw
    bi = (lambda i: 0) if shared else (lambda i: i)
    return pl.pallas_call(
        _conv_kernel,
        grid=(b, s // tc, CONV_DIM // cw),
        in_specs=[
            pl.BlockSpec((1, tc, cw), lambda i, t, c: (bi(i), t, c0 + c)),
            pl.BlockSpec((1, HALO, cw), lambda i, t, c: (bi(i), jnp.maximum(t * hb - 1, 0), c0 + c)),
            pl.BlockSpec((1, HALO, cw), lambda i, t, c: (bi(i), jnp.minimum((t + 1) * hb, nhb - 1), c0 + c)),
            pl.BlockSpec((1, HALO, cw), lambda i, t, c: (i, 0, c)),
            pl.BlockSpec((1, HALO, cw), lambda i, t, c: (i, 0, c)),
            pl.BlockSpec((D_CONV, cw), lambda i, t, c: (0, c)),
            pl.BlockSpec((1, cw), lambda i, t, c: (0, c)),
        ],
        out_specs=pl.BlockSpec((1, tc, cw), lambda i, t, c: (i, t, c)),
        out_shape=jax.ShapeDtypeStruct((b, s, CONV_DIM), BF16),
        compiler_params=_cparams("parallel", "arbitrary", "arbitrary"),
        name="conv_silu",
    )(src, src, src, lh, rh, conv_w, conv_b)


def _softplus(x):
    return jnp.maximum(x, 0.0) + jnp.log1p(jnp.exp(-jnp.abs(x)))


def _ssd_chunks(x_ref, b_ref, c_ref, dtraw_ref, dtb_ref, alog_ref, tri_ref, exp_ref, state_ref, emit,
                *, reverse, n_valid):
    n_seq, L = x_ref.shape[0], x_ref.shape[1]
    off = SSD_HEADS if reverse else 0
    li = lax.broadcasted_iota(jnp.int32, (L, L), 0)
    si = lax.broadcasted_iota(jnp.int32, (L, L), 1)
    keep = (si >= li) if reverse else (si <= li)
    lane = lax.broadcasted_iota(jnp.int32, (L, 2 * SSD_HEAD_DIM), 1)
    left = lane < SSD_HEAD_DIM
    tri = tri_ref[...]

    def per_head(bi):
        dt = _softplus(dtraw_ref[bi] + dtb_ref[...])
        if n_valid < L:
            dt = jnp.where(lax.broadcasted_iota(jnp.int32, dt.shape, 0) < n_valid, dt, 0.0)
        da = dt * (-jnp.exp(alog_ref[...]))
        acs = sum(_dot(tri, piece) for piece in _split3(da))
        tot = acs[0:1] if reverse else acs[L - 1:L]
        src_t = (acs - jnp.log(dt)).T
        decay_out = jnp.exp(acs).astype(BF16)
        decay_in = (jnp.exp(tot - acs) * dt).astype(BF16)
        return acs, src_t, decay_out, decay_in

    heads = [per_head(bi) for bi in range(n_seq)]
    for g in range(SSD_GROUPS):
        gs = slice(g * GROUP_W, (g + 1) * GROUP_W)
        ns = slice(g * SSD_STATE, (g + 1) * SSD_STATE)
        expand = exp_ref[:, gs]
        for bi in range(n_seq):
            acs, src_t, decay_out, decay_in = heads[bi]
            e_out = _dot(decay_out, expand)
            e_tot = e_out[0:1] if reverse else e_out[L - 1:L]
            xw = (x_ref[bi, :, gs].astype(F32) * _dot(decay_in, expand)).astype(BF16)
            bg = b_ref[bi, :, ns]
            cg = c_ref[bi, :, ns]
            cb = _dot_nt(cg, bg)
            st = state_ref[bi, g]
            y_inter = _dot(cg, st.astype(BF16)) * e_out
            state_ref[bi, g] = st * e_tot + _dot_tn(bg, xw)
            pairs = []
            for j in range(SSD_HEADS_PER_GROUP // 2):
                ms = []
                for h in (off + g * SSD_HEADS_PER_GROUP + 2 * j, off + g * SSD_HEADS_PER_GROUP + 2 * j + 1):
                    seg = acs[:, h:h + 1] - src_t[h:h + 1, :]
                    ms.append((cb * jnp.exp(jnp.where(keep, seg, -jnp.inf))).astype(BF16))
                xp = x_ref[bi, :, g * GROUP_W + j * 128:g * GROUP_W + (j + 1) * 128]
                zero = jnp.zeros_like(xp)
                xbd = jnp.concatenate([jnp.where(left, xp, zero), jnp.where(left, zero, xp)], axis=0)
                pairs.append(_dot(jnp.concatenate(ms, axis=1), xbd))
            emit(bi, gs, jnp.concatenate(pairs, axis=1) + y_inter)


def _ssd_fwd_kernel(x_ref, b_ref, c_ref, dtraw_ref, dtb_ref, alog_ref, tri_ref, exp_ref, init_ref,
                    y_ref, fin_ref, state_ref, *, n_valid):
    c = pl.program_id(1)

    @pl.when(c == 0)
    def _():
        state_ref[...] = init_ref[...]

    def emit(bi, gs, y):
        y_ref[bi, :, gs] = y

    _ssd_chunks(x_ref, b_ref, c_ref, dtraw_ref, dtb_ref, alog_ref, tri_ref, exp_ref, state_ref, emit,
                reverse=False, n_valid=n_valid)

    @pl.when(c == pl.num_programs(1) - 1)
    def _():
        fin_ref[...] = state_ref[...]


def _ssd_bwd_kernel(x_ref, b_ref, c_ref, dtraw_ref, dtb_ref, alog_ref, tri_ref, exp_ref,
                    yf_ref, z_ref, dskip_ref, nw_ref, o_ref, state_ref):
    c = pl.program_id(1)

    @pl.when(c == 0)
    def _():
        state_ref[...] = jnp.zeros_like(state_ref)

    def emit(bi, gs, y):
        y = y + yf_ref[bi, :, gs] + x_ref[bi, :, gs].astype(F32) * dskip_ref[:, gs]
        y = y * z_ref[bi, :, gs].astype(F32)
        ms = jnp.mean(y * y, axis=-1, keepdims=True)
        o_ref[bi, :, gs] = (y * lax.rsqrt(ms + EPS) * nw_ref[:, gs]).astype(BF16)

    _ssd_chunks(x_ref, b_ref, c_ref, dtraw_ref, dtb_ref, alog_ref, tri_ref, exp_ref, state_ref, emit,
                reverse=True, n_valid=x_ref.shape[1])


def _seqs_per_step(b):
    return 2 if b % 2 == 0 else 1


def _ssd_common_specs(ns, L, cmap):
    nb = SSD_GROUPS * SSD_STATE
    return [
        pl.BlockSpec((ns, L, SSD_INNER), lambda i, c: (i, cmap(c), 0)),
        pl.BlockSpec((ns, L, nb), lambda i, c: (i, cmap(c), SSD_INNER // nb)),
        pl.BlockSpec((ns, L, nb), lambda i, c: (i, cmap(c), SSD_INNER // nb + 1)),
    ]


def _const_spec(shape):
    nd = len(shape)
    return pl.BlockSpec(shape, lambda i, c: (0,) * nd)


def _ssd_forward(xc, dt_raw, dt_bias, a_log, tri, expand, init, n_valid=CHUNK):
    b, s, _ = xc.shape
    L = CHUNK
    ns = _seqs_per_step(b)
    st_shape = (SSD_GROUPS, SSD_STATE, GROUP_W)
    return pl.pallas_call(
        functools.partial(_ssd_fwd_kernel, n_valid=n_valid),
        grid=(b // ns, s // L),
        in_specs=_ssd_common_specs(ns, L, lambda c: c) + [
            pl.BlockSpec((ns, L, 2 * SSD_HEADS), lambda i, c: (i, c, 0)),
            _const_spec((1, 2 * SSD_HEADS)), _const_spec((1, 2 * SSD_HEADS)),
            _const_spec((L, L)), _const_spec((2 * SSD_HEADS, SSD_INNER)),
            pl.BlockSpec((ns,) + st_shape, lambda i, c: (i, 0, 0, 0)),
        ],
        out_specs=[
            pl.BlockSpec((ns, L, SSD_INNER), lambda i, c: (i, c, 0)),
            pl.BlockSpec((ns,) + st_shape, lambda i, c: (i, 0, 0, 0)),
        ],
        out_shape=[
            jax.ShapeDtypeStruct((b, s, SSD_INNER), F32),
            jax.ShapeDtypeStruct((b,) + st_shape, F32),
        ],
        scratch_shapes=[pltpu.VMEM((ns,) + st_shape, F32)],
        compiler_params=_cparams("parallel", "arbitrary"),
        name="ssd_forward",
    )(xc, xc, xc, dt_raw, dt_bias, a_log, tri, expand, init)


def _ssd_backward(xc, dt_raw, dt_bias, a_log, tri, expand, y_f, proj3, d_skip, norm_w):
    b, s, _ = xc.shape
    L = CHUNK
    nc = s // L
    ns = _seqs_per_step(b)
    rev = lambda c: nc - 1 - c
    return pl.pallas_call(
        _ssd_bwd_kernel,
        grid=(b // ns, nc),
        in_specs=_ssd_common_specs(ns, L, rev) + [
            pl.BlockSpec((ns, L, 2 * SSD_HEADS), lambda i, c: (i, rev(c), 0)),
            _const_spec((1, 2 * SSD_HEADS)), _const_spec((1, 2 * SSD_HEADS)),
            _const_spec((L, L)), _const_spec((2 * SSD_HEADS, SSD_INNER)),
            pl.BlockSpec((ns, L, SSD_INNER), lambda i, c: (i, rev(c), 0)),
            pl.BlockSpec((ns, L, SSD_INNER), lambda i, c: (i, rev(c), COL_Z // SSD_INNER)),
            _const_spec((1, SSD_INNER)), _const_spec((1, SSD_INNER)),
        ],
        out_specs=pl.BlockSpec((ns, L, SSD_INNER), lambda i, c: (i, rev(c), 0)),
        out_shape=jax.ShapeDtypeStruct((b, s, SSD_INNER), BF16),
        scratch_shapes=[pltpu.VMEM((ns, SSD_GROUPS, SSD_STATE, GROUP_W), F32)],
        compiler_params=_cparams("parallel", "arbitrary"),
        name="ssd_backward",
    )(xc, xc, xc, dt_raw, dt_bias, a_log, tri, expand, y_f, proj3, d_skip, norm_w)


def _outproj_kernel(a_ref, s_ref, wa_ref, ws_ref, x_ref, o_ref):
    o_ref[...] = x_ref[...] + _dot(a_ref[...], wa_ref[...]) + _dot(s_ref[...], ws_ref[...])


def _out_proj(attn2, ssd2, w_a, w_s, x2d, tm, tn):
    m = x2d.shape[0]
    return pl.pallas_call(
        _outproj_kernel,
        grid=(D_MODEL // tn, m // tm),
        in_specs=[
            pl.BlockSpec((tm, ATTN_W), lambda j, i: (i, 0)),
            pl.BlockSpec((tm, SSD_INNER), lambda j, i: (i, 0)),
            pl.BlockSpec((ATTN_W, tn), lambda j, i: (0, j)),
            pl.BlockSpec((SSD_INNER, tn), lambda j, i: (0, j)),
            pl.BlockSpec((tm, tn), lambda j, i: (i, j)),
        ],
        out_specs=pl.BlockSpec((tm, tn), lambda j, i: (i, j)),
        out_shape=jax.ShapeDtypeStruct((m, D_MODEL), F32),
        compiler_params=_cparams("parallel", "parallel"),
        name="out_proj",
    )(attn2, ssd2, w_a, w_s, x2d)


def _ffn_kernel(h_ref, g_ref, wg_ref, wu_ref, wd_ref, o_ref, f_ref):
    j = pl.program_id(1)

    @pl.when(j == 0)
    def _():
        h = h_ref[...]
        ms = jnp.mean(h * h, axis=-1, keepdims=True)
        f_ref[...] = (h * lax.rsqrt(ms + EPS) * g_ref[...]).astype(BF16)
        o_ref[...] = h

    f = f_ref[...]
    gate = _dot(f, wg_ref[...])
    up = _dot(f, wu_ref[...])
    o_ref[...] += _dot((_silu(gate) * up).astype(BF16), wd_ref[...])


def _ffn(h2d, g_ffn, w_gate, w_up, w_down, tm, tf):
    m = h2d.shape[0]
    return pl.pallas_call(
        _ffn_kernel,
        grid=(m // tm, D_FF // tf),
        in_specs=[
            pl.BlockSpec((tm, D_MODEL), lambda i, j: (i, 0)),
            pl.BlockSpec((1, D_MODEL), lambda i, j: (0, 0)),
            pl.BlockSpec((D_MODEL, tf), lambda i, j: (0, j)),
            pl.BlockSpec((D_MODEL, tf), lambda i, j: (0, j)),
            pl.BlockSpec((tf, D_MODEL), lambda i, j: (j, 0)),
        ],
        out_specs=pl.BlockSpec((tm, D_MODEL), lambda i, j: (i, 0)),
        out_shape=jax.ShapeDtypeStruct((m, D_MODEL), F32),
        scratch_shapes=[pltpu.VMEM((tm, D_MODEL), BF16)],
        compiler_params=_cparams("parallel", "arbitrary"),
        name="ffn",
    )(h2d, g_ffn, w_gate, w_up, w_down)


_EXPAND = np.kron(np.eye(SSD_HEADS), np.ones((1, SSD_HEAD_DIM)))


def _prepare(meta_tokens, g_mix, w_in, q_norm, k_norm, rpb, conv_w, conv_b, dt_bias_f, dt_bias_b,
             a_log_f, a_log_b, d_skip, ssd_norm, w_out, g_ffn, w_gate, w_up, w_down):
    l = 0
    p = dict(
        g_mix=g_mix[l].reshape(1, D_MODEL),
        w_in=w_in[l].astype(BF16),
        q_gain=jnp.tile(q_norm[l].astype(F32) * HEAD_DIM ** -0.5, IN_TN // HEAD_DIM).reshape(1, IN_TN),
        k_gain=jnp.tile(k_norm[l].astype(F32), N_KV_HEADS).reshape(1, KV_W),
        ones_bd=jnp.asarray(np.kron(np.eye(256 // HEAD_DIM), np.ones((HEAD_DIM, HEAD_DIM))), BF16),
        bias_tab=_bias_table(rpb[l]),
        conv_w=conv_w[l].astype(F32),
        conv_b=conv_b[l].astype(F32).reshape(1, CONV_DIM),
        dt_bias=jnp.concatenate([dt_bias_f[l], dt_bias_b[l]]).astype(F32).reshape(1, 2 * SSD_HEADS),
        a_log=jnp.concatenate([a_log_f[l], a_log_b[l]]).astype(F32).reshape(1, 2 * SSD_HEADS),
        d_skip=jnp.repeat(d_skip[l].astype(F32), SSD_HEAD_DIM).reshape(1, SSD_INNER),
        ssd_norm=ssd_norm[l].astype(F32).reshape(1, SSD_INNER),
        tril=jnp.asarray(np.tril(np.ones((CHUNK, CHUNK))), BF16),
        triu=jnp.asarray(np.triu(np.ones((CHUNK, CHUNK))), BF16),
        expand_f=jnp.asarray(np.concatenate([_EXPAND, 0 * _EXPAND]), BF16),
        expand_b=jnp.asarray(np.concatenate([0 * _EXPAND, _EXPAND]), BF16),
        w_out_a=w_out[l, :ATTN_W].astype(BF16),
        w_out_s=w_out[l, ATTN_W:].astype(BF16),
        g_ffn=g_ffn[l].reshape(1, D_MODEL),
        w_gate=w_gate[l].astype(BF16),
        w_up=w_up[l].astype(BF16),
        w_down=w_down[l].astype(BF16),
    )
    proj_m, dt_m = _in_proj(meta_tokens.astype(F32), p["g_mix"], p["w_in"], p["q_gain"],
                            p["k_gain"], p["ones_bd"], tm=N_META)
    p["k_meta"] = proj_m[:, COL_K:COL_K + KV_W]
    p["v_meta"] = proj_m[:, COL_V:COL_V + KV_W]
    p["proj_meta"] = proj_m
    p["dt_meta"] = dt_m
    return p


def _trunk(x, p):
    b, s, _ = x.shape
    m = b * s
    x2d = x.reshape(m, D_MODEL)
    proj, dt_raw = _in_proj(x2d, p["g_mix"], p["w_in"], p["q_gain"], p["k_gain"], p["ones_bd"], tm=1024)
    proj3 = proj.reshape(b, s, PROJ_W)
    dt3 = dt_raw.reshape(b, s, 2 * SSD_HEADS)

    attn = _attention(proj3, p["bias_tab"], p["k_meta"], p["v_meta"])

    xbc_meta = p["proj_meta"][:, COL_XBC:COL_XBC + CONV_DIM]
    zeros_h = jnp.zeros((b, HALO, CONV_DIM), BF16)
    xc = _conv(proj3, COL_XBC, jnp.broadcast_to(xbc_meta[None], (b, HALO, CONV_DIM)), zeros_h,
               p["conv_w"], p["conv_b"], tc=512, cw=1024, shared=False)
    meta_in = jnp.concatenate([jnp.broadcast_to(xbc_meta[None], (b, N_META, CONV_DIM)),
                               proj3[:, :CHUNK - N_META, COL_XBC:COL_XBC + CONV_DIM]], axis=1)
    xc_meta = _conv(meta_in, 0, zeros_h, zeros_h, p["conv_w"], p["conv_b"], tc=CHUNK, cw=1024, shared=False)

    dt_meta = jnp.broadcast_to(jnp.pad(p["dt_meta"], ((0, CHUNK - N_META), (0, 0)))[None],
                               (b, CHUNK, 2 * SSD_HEADS))
    zero_state = jnp.zeros((b, SSD_GROUPS, SSD_STATE, GROUP_W), F32)
    _, state0 = _ssd_forward(xc_meta, dt_meta, p["dt_bias"], p["a_log"], p["tril"], p["expand_f"],
                             zero_state, n_valid=N_META)
    y_f, _ = _ssd_forward(xc, dt3, p["dt_bias"], p["a_log"], p["tril"], p["expand_f"], state0)
    ssd = _ssd_backward(xc, dt3, p["dt_bias"], p["a_log"], p["triu"], p["expand_b"], y_f, proj3,
                        p["d_skip"], p["ssd_norm"])

    h1 = _out_proj(attn.reshape(m, ATTN_W), ssd.reshape(m, SSD_INNER), p["w_out_a"], p["w_out_s"], x2d,
                   tm=512, tn=1024)
    out = _ffn(h1, p["g_ffn"], p["w_gate"], p["w_up"], p["w_down"], tm=512, tf=512)
    return out.reshape(b, s, D_MODEL)


def kernel(x_prompt, x_sample, meta_tokens, g_mix, w_in, q_norm, k_norm, rpb, conv_w, conv_b, dt_bias_f, dt_bias_b, a_log_f, a_log_b, d_skip, ssd_norm, w_out, g_ffn, w_gate, w_up, w_down):
    p = _prepare(meta_tokens, g_mix, w_in, q_norm, k_norm, rpb, conv_w, conv_b, dt_bias_f, dt_bias_b,
                 a_log_f, a_log_b, d_skip, ssd_norm, w_out, g_ffn, w_gate, w_up, w_down)
    return (_trunk(x_prompt, p), _trunk(x_sample, p))
```

```python
import functools

import numpy as np
import jax
import jax.numpy as jnp
from jax import lax
from jax.experimental import pallas as pl
from jax.experimental.pallas import tpu as pltpu

F32 = jnp.float32
BF16 = jnp.bfloat16

D_MODEL = 2048
N_META = 16
GRID_W = 64
WIN_ROWS = 8
WIN_COLS = 16
N_Q_HEADS = 32
N_KV_HEADS = 8
Q_PER_KV = N_Q_HEADS // N_KV_HEADS
HEAD_DIM = 64
ATTN_W = N_Q_HEADS * HEAD_DIM
KV_W = N_KV_HEADS * HEAD_DIM
SSD_INNER = 2 * D_MODEL
SSD_HEAD_DIM = 64
SSD_HEADS = SSD_INNER // SSD_HEAD_DIM
SSD_GROUPS = 8
SSD_HEADS_PER_GROUP = SSD_HEADS // SSD_GROUPS
SSD_STATE = 128
D_CONV = 4
CHUNK = 128
CONV_DIM = SSD_INNER + 2 * SSD_GROUPS * SSD_STATE
D_FF = 5632
EPS = 1e-6

GROUP_W = SSD_HEADS_PER_GROUP * SSD_HEAD_DIM
PROJ_W = ATTN_W + 2 * KV_W + SSD_INNER + CONV_DIM
COL_Z = 0
COL_XBC = SSD_INNER
COL_Q = COL_XBC + CONV_DIM
COL_K = COL_Q + ATTN_W
COL_V = COL_K + KV_W
NEG = -1e30
VMEM_LIMIT = 56 * 1024 * 1024


def _cparams(*sem):
    return pltpu.CompilerParams(dimension_semantics=sem, vmem_limit_bytes=VMEM_LIMIT)


def _dot(a, b):
    return jnp.dot(a, b, preferred_element_type=F32)


def _dot_nt(a, b):
    return lax.dot_general(a, b, (((1,), (1,)), ((), ())), preferred_element_type=F32)


def _dot_tn(a, b):
    return lax.dot_general(a, b, (((0,), (0,)), ((), ())), preferred_element_type=F32)


def _split2(x):
    hi = x.astype(BF16)
    lo = (x - hi.astype(F32)).astype(BF16)
    return hi, lo


def _split3(x):
    hi = x.astype(BF16)
    r = x - hi.astype(F32)
    mid = r.astype(BF16)
    lo = (r - mid.astype(F32)).astype(BF16)
    return hi, mid, lo


NEG_LOG2_E = -1.4426950408889634


def _silu(x):
    return x / (1.0 + jnp.exp2(x * NEG_LOG2_E))


IN_TN = 1024
N_QK_TILES = ATTN_W // IN_TN
KV_TILE = N_QK_TILES
Z_LAST_TILE = KV_TILE + SSD_INNER // IN_TN


MXU_W = 256


def _head_norm(a, gain, ones_bd):
    hi, lo = _split2(a * a)
    ms = _dot(hi, ones_bd) + _dot(lo, ones_bd)
    return a * lax.rsqrt(ms + EPS) * gain


def _inproj_kernel(x_ref, g_ref, w_ref, wdt_ref, qg_ref, kg_ref, ones_ref, o_ref, dt_ref, u_ref):
    j = pl.program_id(1)

    @pl.when(j == 0)
    def _():
        x = x_ref[...]
        ms = jnp.mean(x * x, axis=-1, keepdims=True)
        u = (x * lax.rsqrt(ms + EPS) * g_ref[...]).astype(BF16)
        u_ref[...] = u
        dt_ref[...] = _dot(u, wdt_ref[...])

    def emit(finish, split):
        u = u_ref[...]
        acc = None if split else _dot(u, w_ref[...])
        for c in range(IN_TN // MXU_W):
            cs = slice(c * MXU_W, (c + 1) * MXU_W)
            a = _dot(u, w_ref[:, cs]) if split else acc[:, cs]
            o_ref[:, cs] = finish(c, cs, a).astype(BF16)

    @pl.when(j < N_QK_TILES)
    def _():
        emit(lambda c, cs, a: _head_norm(a, qg_ref[:, cs], ones_ref[...]), split=False)

    @pl.when(j == KV_TILE)
    def _():
        emit(lambda c, cs, a: _head_norm(a, kg_ref[:, cs], ones_ref[...]) if c < KV_W // MXU_W else a,
             split=False)

    @pl.when((j > KV_TILE) & (j <= Z_LAST_TILE))
    def _():
        emit(lambda c, cs, a: _silu(a), split=True)

    @pl.when(j > Z_LAST_TILE)
    def _():
        emit(lambda c, cs, a: a, split=True)


def _in_proj(x2d, g_mix, w_in, q_gain, k_gain, ones_bd, tm):
    m = x2d.shape[0]
    n_tiles = PROJ_W // IN_TN
    dst = lambda j: jnp.where(j <= KV_TILE, j + (COL_Q // IN_TN), j - (KV_TILE + 1))
    return pl.pallas_call(
        _inproj_kernel,
        grid=(m // tm, n_tiles),
        in_specs=[
            pl.BlockSpec((tm, D_MODEL), lambda i, j: (i, 0)),
            pl.BlockSpec((1, D_MODEL), lambda i, j: (0, 0)),
            pl.BlockSpec((D_MODEL, IN_TN), lambda i, j: (0, j)),
            pl.BlockSpec((D_MODEL, 2 * SSD_HEADS), lambda i, j: (0, PROJ_W // (2 * SSD_HEADS))),
            pl.BlockSpec((1, IN_TN), lambda i, j: (0, 0)),
            pl.BlockSpec((1, KV_W), lambda i, j: (0, 0)),
            pl.BlockSpec((256, 256), lambda i, j: (0, 0)),
        ],
        out_specs=[
            pl.BlockSpec((tm, IN_TN), lambda i, j: (i, dst(j))),
            pl.BlockSpec((tm, 2 * SSD_HEADS), lambda i, j: (i, 0)),
        ],
        out_shape=[
            jax.ShapeDtypeStruct((m, PROJ_W), BF16),
            jax.ShapeDtypeStruct((m, 2 * SSD_HEADS), F32),
        ],
        scratch_shapes=[pltpu.VMEM((tm, D_MODEL), BF16)],
        compiler_params=_cparams("parallel", "arbitrary"),
        name="in_proj",
    )(x2d, g_mix, w_in, w_in, q_gain, k_gain, ones_bd)


N_BAND = WIN_ROWS * GRID_W
HALF_W = GRID_W // 2
ATTN_ROWS = 4
META_AT = {0: GRID_W - N_META, 1: 0}


def _swap_halves(t):
    return jnp.concatenate([t[:, HEAD_DIM:], t[:, :HEAD_DIM]], axis=1)


def _with_meta(band, meta, hf):
    at = META_AT[hf]
    parts = [band[:at]] if at else []
    return jnp.concatenate(parts + [meta, band[at + N_META:]], axis=0)


def _attn_kernel(q_ref, k_ref, v_ref, bias_ref, km_ref, vm_ref, o_ref, *, rows):
    n_rows = q_ref.shape[1] // GRID_W
    r = [pl.program_id(1) * n_rows + ri for ri in range(n_rows)]
    rs = [jnp.clip(x - WIN_ROWS // 2, 0, rows - WIN_ROWS) for x in r]
    start = [pl.multiple_of(x * GRID_W, GRID_W) for x in rs]
    d0 = [WIN_ROWS - 1 - (x - y) for x, y in zip(r, rs)]
    lo = lax.broadcasted_iota(jnp.int32, (HALF_W, 2 * HEAD_DIM), 1) < HEAD_DIM
    hi = jnp.logical_not(lo)
    ones = jnp.ones((N_BAND, 2 * HEAD_DIM), BF16)
    units = [(ri, kp, hf) for ri in range(n_rows) for kp in range(N_KV_HEADS // 2) for hf in range(2)]

    def scores(ri, kp, hf):
        ps = slice(kp * 2 * HEAD_DIM, (kp + 1) * 2 * HEAD_DIM)
        kb = _with_meta(k_ref[0, pl.ds(start[ri], N_BAND), ps], km_ref[:, ps], hf)
        q0 = ri * GRID_W + hf * HALF_W
        tiles = []
        for par in range(2):
            keep = lo if par == 0 else hi
            for t in range(Q_PER_KV // 2):
                c0 = ((2 * kp + par) * 2 + t) * 128
                qt = q_ref[0, q0:q0 + HALF_W, c0:c0 + 128]
                qs = _swap_halves(qt)
                for piece in ((qt, qs) if par == 0 else (qs, qt)):
                    tiles.append(jnp.where(keep, piece, jnp.zeros_like(piece)))
        return _dot_nt(jnp.concatenate(tiles, axis=0), kb)

    lane = lax.broadcasted_iota(jnp.int32, (1, 2 * HEAD_DIM), 1)
    s_next = scores(*units[0])
    for idx, (ri, kp, hf) in enumerate(units):
        s_raw = s_next
        if idx + 1 < len(units):
            s_next = scores(*units[idx + 1])
        ps = slice(kp * 2 * HEAD_DIM, (kp + 1) * 2 * HEAD_DIM)
        vb = _with_meta(v_ref[0, pl.ds(start[ri], N_BAND), ps], vm_ref[:, ps], hf)
        q0 = ri * GRID_W + hf * HALF_W
        s = []
        for j in range(WIN_ROWS // 2):
            bias = bias_ref[kp, hf, d0[ri] + 2 * j]
            if j == 0:
                at = META_AT[hf]
                bias = jnp.where((lane >= at) & (lane < at + N_META), 0.0, bias)
            s.append(s_raw[:, j * 128:(j + 1) * 128] + bias)
        m = jnp.max(jnp.maximum(jnp.maximum(s[0], s[1]), jnp.maximum(s[2], s[3])), axis=1, keepdims=True)
        p = jnp.concatenate([jnp.exp(sj - m) for sj in s], axis=1).astype(BF16)
        oa = _dot(p, jnp.concatenate([vb, ones], axis=1))
        o = oa[:, :128] / oa[:, 128:]
        for par in range(2):
            for t in range(Q_PER_KV // 2):
                base = (par * Q_PER_KV + 2 * t) * HALF_W
                a = o[base:base + HALF_W]
                b = o[base + HALF_W:base + 2 * HALF_W]
                if par == 0:
                    tile = jnp.where(lo, a, pltpu.roll(b, HEAD_DIM, 1))
                else:
                    tile = jnp.where(lo, pltpu.roll(a, HEAD_DIM, 1), b)
                c0 = ((2 * kp + par) * 2 + t) * 128
                o_ref[0, q0:q0 + HALF_W, c0:c0 + 128] = tile.astype(BF16)


def _attention(proj3, bias_tab, k_meta, v_meta):
    b, s, _ = proj3.shape
    rows = s // GRID_W
    return pl.pallas_call(
        functools.partial(_attn_kernel, rows=rows),
        grid=(b, rows // ATTN_ROWS),
        in_specs=[
            pl.BlockSpec((1, ATTN_ROWS * GRID_W, ATTN_W), lambda i, r: (i, r, COL_Q // ATTN_W)),
            pl.BlockSpec((1, s, KV_W), lambda i, r: (i, 0, COL_K // KV_W)),
            pl.BlockSpec((1, s, KV_W), lambda i, r: (i, 0, COL_V // KV_W)),
            pl.BlockSpec(bias_tab.shape, lambda i, r: (0, 0, 0, 0, 0), pipeline_mode=pl.Buffered(1)),
            pl.BlockSpec((N_META, KV_W), lambda i, r: (0, 0)),
            pl.BlockSpec((N_META, KV_W), lambda i, r: (0, 0)),
        ],
        out_specs=pl.BlockSpec((1, ATTN_ROWS * GRID_W, ATTN_W), lambda i, r: (i, r, 0)),
        out_shape=jax.ShapeDtypeStruct((b, s, ATTN_W), BF16),
        compiler_params=_cparams("parallel", "arbitrary"),
        name="na_attention",
    )(proj3, proj3, proj3, bias_tab, k_meta, v_meta)


def _bias_table(rpb):
    c = np.arange(GRID_W)
    cs = np.clip(c - WIN_COLS // 2, 0, GRID_W - WIN_COLS)
    inwin = (c[None, :] >= cs[:, None]) & (c[None, :] < cs[:, None] + WIN_COLS)
    dc = np.clip(c[None, :] - c[:, None] + WIN_COLS - 1, 0, 2 * WIN_COLS - 2)
    pick = jnp.asarray(dc[None] == np.arange(2 * WIN_COLS - 1)[:, None, None], F32)
    slab = jnp.einsum("hdk,kcj->hdcj", rpb.astype(F32), pick, precision=lax.Precision.HIGHEST)
    slab = jnp.where(inwin[None, None], slab, NEG)
    nd = 2 * WIN_ROWS - 1
    slab = slab.reshape(N_KV_HEADS // 2, 2 * Q_PER_KV, nd, 2, HALF_W, GRID_W)
    slab = slab.transpose(0, 3, 2, 1, 4, 5).reshape(N_KV_HEADS // 2, 2, nd, 2 * Q_PER_KV * HALF_W, GRID_W)
    return jnp.concatenate([slab[:, :, :nd - 1], slab[:, :, 1:]], axis=-1)


HALO = 16
CONV_SUB = 128
CONV_K = 256
SHIFT_TAPS = (0, 1, 3)


def _shift_matrix():
    m = np.zeros((len(SHIFT_TAPS) * CONV_SUB, CONV_K), np.float32)
    for k, tap in enumerate(SHIFT_TAPS):
        t = np.arange(CONV_SUB)
        m[k * CONV_SUB + t, t + HALO + tap - D_CONV // 2] = 1.0
    return m


def _conv_kernel(cur_ref, prev_ref, next_ref, lh_ref, rh_ref, w_ref, b_ref, t_ref, o_ref):
    i = pl.program_id(1)
    last = pl.num_programs(1) - 1
    tc, cw = cur_ref.shape[1], cur_ref.shape[2]
    before = jnp.where(i == 0, lh_ref[0], prev_ref[0])
    after = jnp.where(i == last, rh_ref[0], next_ref[0])
    pad = jnp.zeros((CONV_K - CONV_SUB - 2 * HALO, cw), BF16)
    ext = jnp.concatenate([before, cur_ref[0], after, pad], axis=0)
    w = w_ref[...]
    shift = t_ref[...]
    for blk in range(tc // CONV_SUB):
        r0 = blk * CONV_SUB
        sh = _dot(shift, ext[r0:r0 + CONV_K])
        y = b_ref[...] + w[D_CONV // 2:D_CONV // 2 + 1] * cur_ref[0, r0:r0 + CONV_SUB].astype(F32)
        for k, tap in enumerate(SHIFT_TAPS):
            y = y + w[tap:tap + 1] * sh[k * CONV_SUB:(k + 1) * CONV_SUB]
        o_ref[0, r0:r0 + CONV_SUB] = _silu(y).astype(BF16)


def _conv(src, col0, lh, rh, conv_w, conv_b, tc, cw, shared):
    b = lh.shape[0]
    s = src.shape[1]
    hb = tc // HALO
    nhb = s // HALO
    c0 = col0 // cw
    shift = jnp.asarray(_shift_matrix(), BF16)
    bi = (lambda i: 0) if shared else (lambda i: i)
    return pl.pallas_call(
        _conv_kernel,
        grid=(b, s // tc, CONV_DIM // cw),
        in_specs=[
            pl.BlockSpec((1, tc, cw), lambda i, t, c: (bi(i), t, c0 + c)),
            pl.BlockSpec((1, HALO, cw), lambda i, t, c: (bi(i), jnp.maximum(t * hb - 1, 0), c0 + c)),
            pl.BlockSpec((1, HALO, cw), lambda i, t, c: (bi(i), jnp.minimum((t + 1) * hb, nhb - 1), c0 + c)),
            pl.BlockSpec((1, HALO, cw), lambda i, t, c: (i, 0, c)),
            pl.BlockSpec((1, HALO, cw), lambda i, t, c: (i, 0, c)),
            pl.BlockSpec((D_CONV, cw), lambda i, t, c: (0, c)),
            pl.BlockSpec((1, cw), lambda i, t, c: (0, c)),
            pl.BlockSpec(shift.shape, lambda i, t, c: (0, 0)),
        ],
        out_specs=pl.BlockSpec((1, tc, cw), lambda i, t, c: (i, t, c)),
        out_shape=jax.ShapeDtypeStruct((b, s, CONV_DIM), BF16),
        compiler_params=_cparams("parallel", "arbitrary", "arbitrary"),
        name="conv_silu",
    )(src, src, src, lh, rh, conv_w, conv_b, shift)


def _softplus(x):
    return jnp.maximum(x, 0.0) + jnp.log1p(jnp.exp(-jnp.abs(x)))


def _ssd_chunks(x_ref, b_ref, c_ref, dtraw_ref, dtb_ref, alog_ref, tri_ref, exp_ref, state_ref, emit,
                *, reverse, n_valid):
    n_seq, L = x_ref.shape[0], x_ref.shape[1]
    off = SSD_HEADS if reverse else 0
    li = lax.broadcasted_iota(jnp.int32, (L, L), 0)
    si = lax.broadcasted_iota(jnp.int32, (L, L), 1)
    keep = (si >= li) if reverse else (si <= li)
    lane = lax.broadcasted_iota(jnp.int32, (L, 2 * SSD_HEAD_DIM), 1)
    left = lane < SSD_HEAD_DIM
    tri = tri_ref[...]

    def per_head(bi):
        dt = _softplus(dtraw_ref[bi] + dtb_ref[...])
        if n_valid < L:
            dt = jnp.where(lax.broadcasted_iota(jnp.int32, dt.shape, 0) < n_valid, dt, 0.0)
        da = dt * (-jnp.exp(alog_ref[...]))
        acs = sum(_dot(tri, piece) for piece in _split3(da))
        tot = acs[0:1] if reverse else acs[L - 1:L]
        src_t = (acs - jnp.log(dt)).T
        decay_out = jnp.exp(acs).astype(BF16)
        decay_in = (jnp.exp(tot - acs) * dt).astype(BF16)
        return acs, src_t, decay_out, decay_in

    heads = [per_head(bi) for bi in range(n_seq)]
    for g in range(SSD_GROUPS):
        gs = slice(g * GROUP_W, (g + 1) * GROUP_W)
        ns = slice(g * SSD_STATE, (g + 1) * SSD_STATE)
        expand = exp_ref[:, gs]
        for bi in range(n_seq):
            acs, src_t, decay_out, decay_in = heads[bi]
            e_out = _dot(decay_out, expand)
            e_tot = e_out[0:1] if reverse else e_out[L - 1:L]
            xw = (x_ref[bi, :, gs].astype(F32) * _dot(decay_in, expand)).astype(BF16)
            bg = b_ref[bi, :, ns]
            cg = c_ref[bi, :, ns]
            cb = _dot_nt(cg, bg)
            st = state_ref[bi, g]
            y_inter = _dot(cg, st.astype(BF16)) * e_out
            state_ref[bi, g] = st * e_tot + _dot_tn(bg, xw)
            pairs = []
            for j in range(SSD_HEADS_PER_GROUP // 2):
                ms = []
                for h in (off + g * SSD_HEADS_PER_GROUP + 2 * j, off + g * SSD_HEADS_PER_GROUP + 2 * j + 1):
                    seg = acs[:, h:h + 1] - src_t[h:h + 1, :]
                    ms.append((cb * jnp.exp(jnp.where(keep, seg, -jnp.inf))).astype(BF16))
                xp = x_ref[bi, :, g * GROUP_W + j * 128:g * GROUP_W + (j + 1) * 128]
                zero = jnp.zeros_like(xp)
                xbd = jnp.concatenate([jnp.where(left, xp, zero), jnp.where(left, zero, xp)], axis=0)
                pairs.append(_dot(jnp.concatenate(ms, axis=1), xbd))
            emit(bi, gs, jnp.concatenate(pairs, axis=1) + y_inter)


def _ssd_fwd_kernel(x_ref, b_ref, c_ref, dtraw_ref, dtb_ref, alog_ref, tri_ref, exp_ref, init_ref,
                    y_ref, fin_ref, state_ref, *, n_valid):
    c = pl.program_id(1)

    @pl.when(c == 0)
    def _():
        state_ref[...] = init_ref[...]

    def emit(bi, gs, y):
        y_ref[bi, :, gs] = y

    _ssd_chunks(x_ref, b_ref, c_ref, dtraw_ref, dtb_ref, alog_ref, tri_ref, exp_ref, state_ref, emit,
                reverse=False, n_valid=n_valid)

    @pl.when(c == pl.num_programs(1) - 1)
    def _():
        fin_ref[...] = state_ref[...]


def _ssd_bwd_kernel(x_ref, b_ref, c_ref, dtraw_ref, dtb_ref, alog_ref, tri_ref, exp_ref,
                    yf_ref, z_ref, dskip_ref, nw_ref, o_ref, state_ref):
    c = pl.program_id(1)

    @pl.when(c == 0)
    def _():
        state_ref[...] = jnp.zeros_like(state_ref)

    def emit(bi, gs, y):
        y = y + yf_ref[bi, :, gs] + x_ref[bi, :, gs].astype(F32) * dskip_ref[:, gs]
        y = y * z_ref[bi, :, gs].astype(F32)
        ms = jnp.mean(y * y, axis=-1, keepdims=True)
        o_ref[bi, :, gs] = (y * lax.rsqrt(ms + EPS) * nw_ref[:, gs]).astype(BF16)

    _ssd_chunks(x_ref, b_ref, c_ref, dtraw_ref, dtb_ref, alog_ref, tri_ref, exp_ref, state_ref, emit,
                reverse=True, n_valid=x_ref.shape[1])


def _seqs_per_step(b):
    return 2 if b % 2 == 0 else 1


def _ssd_common_specs(ns, L, cmap):
    nb = SSD_GROUPS * SSD_STATE
    return [
        pl.BlockSpec((ns, L, SSD_INNER), lambda i, c: (i, cmap(c), 0)),
        pl.BlockSpec((ns, L, nb), lambda i, c: (i, cmap(c), SSD_INNER // nb)),
        pl.BlockSpec((ns, L, nb), lambda i, c: (i, cmap(c), SSD_INNER // nb + 1)),
    ]


def _const_spec(shape):
    nd = len(shape)
    return pl.BlockSpec(shape, lambda i, c: (0,) * nd)


def _ssd_forward(xc, dt_raw, dt_bias, a_log, tri, expand, init, n_valid=CHUNK):
    b, s, _ = xc.shape
    L = CHUNK
    ns = _seqs_per_step(b)
    st_shape = (SSD_GROUPS, SSD_STATE, GROUP_W)
    nc = s // L
    return pl.pallas_call(
        functools.partial(_ssd_fwd_kernel, n_valid=n_valid),
        grid=(b // ns, nc),
        in_specs=_ssd_common_specs(ns, L, lambda c: c) + [
            pl.BlockSpec((ns, L, 2 * SSD_HEADS), lambda i, c: (i, c, 0)),
            _const_spec((1, 2 * SSD_HEADS)), _const_spec((1, 2 * SSD_HEADS)),
            _const_spec((L, L)), _const_spec((2 * SSD_HEADS, SSD_INNER)),
            pl.BlockSpec((ns,) + st_shape, lambda i, c: (i, 0, 0, 0)),
        ],
        out_specs=[
            pl.BlockSpec((ns, L, SSD_INNER), lambda i, c: (i, c, 0)),
            pl.BlockSpec((ns,) + st_shape, lambda i, c: (i, 0, 0, 0)),
        ],
        out_shape=[
            jax.ShapeDtypeStruct((b, s, SSD_INNER), F32),
            jax.ShapeDtypeStruct((b,) + st_shape, F32),
        ],
        scratch_shapes=[pltpu.VMEM((ns,) + st_shape, F32)],
        compiler_params=_cparams("parallel", "arbitrary"),
        name="ssd_forward",
    )(xc, xc, xc, dt_raw, dt_bias, a_log, tri, expand, init)


def _ssd_backward(xc, dt_raw, dt_bias, a_log, tri, expand, y_f, proj3, d_skip, norm_w):
    b, s, _ = xc.shape
    L = CHUNK
    nc = s // L
    ns = _seqs_per_step(b)
    rev = lambda c: nc - 1 - c
    return pl.pallas_call(
        _ssd_bwd_kernel,
        grid=(b // ns, nc),
        in_specs=_ssd_common_specs(ns, L, rev) + [
            pl.BlockSpec((ns, L, 2 * SSD_HEADS), lambda i, c: (i, rev(c), 0)),
            _const_spec((1, 2 * SSD_HEADS)), _const_spec((1, 2 * SSD_HEADS)),
            _const_spec((L, L)), _const_spec((2 * SSD_HEADS, SSD_INNER)),
            pl.BlockSpec((ns, L, SSD_INNER), lambda i, c: (i, rev(c), 0)),
            pl.BlockSpec((ns, L, SSD_INNER), lambda i, c: (i, rev(c), COL_Z // SSD_INNER)),
            _const_spec((1, SSD_INNER)), _const_spec((1, SSD_INNER)),
        ],
        out_specs=pl.BlockSpec((ns, L, SSD_INNER), lambda i, c: (i, rev(c), 0)),
        out_shape=jax.ShapeDtypeStruct((b, s, SSD_INNER), BF16),
        scratch_shapes=[pltpu.VMEM((ns, SSD_GROUPS, SSD_STATE, GROUP_W), F32)],
        compiler_params=_cparams("parallel", "arbitrary"),
        name="ssd_backward",
    )(xc, xc, xc, dt_raw, dt_bias, a_log, tri, expand, y_f, proj3, d_skip, norm_w)


def _outproj_kernel(a_ref, s_ref, wa_ref, ws_ref, x_ref, o_ref):
    o_ref[...] = x_ref[...] + _dot(a_ref[...], wa_ref[...]) + _dot(s_ref[...], ws_ref[...])


def _out_proj(attn2, ssd2, w_a, w_s, x2d, tm, tn):
    m = x2d.shape[0]
    return pl.pallas_call(
        _outproj_kernel,
        grid=(D_MODEL // tn, m // tm),
        in_specs=[
            pl.BlockSpec((tm, ATTN_W), lambda j, i: (i, 0)),
            pl.BlockSpec((tm, SSD_INNER), lambda j, i: (i, 0)),
            pl.BlockSpec((ATTN_W, tn), lambda j, i: (0, j)),
            pl.BlockSpec((SSD_INNER, tn), lambda j, i: (0, j)),
            pl.BlockSpec((tm, tn), lambda j, i: (i, j)),
        ],
        out_specs=pl.BlockSpec((tm, tn), lambda j, i: (i, j)),
        out_shape=jax.ShapeDtypeStruct((m, D_MODEL), F32),
        compiler_params=_cparams("parallel", "parallel"),
        name="out_proj",
    )(attn2, ssd2, w_a, w_s, x2d)


def _ffn_kernel(h_ref, g_ref, wg_ref, wu_ref, wd_ref, o_ref, f_ref):
    j = pl.program_id(1)

    @pl.when(j == 0)
    def _():
        h = h_ref[...]
        ms = jnp.mean(h * h, axis=-1, keepdims=True)
        f_ref[...] = (h * lax.rsqrt(ms + EPS) * g_ref[...]).astype(BF16)
        o_ref[...] = h

    f = f_ref[...]
    gate = _dot(f, wg_ref[...])
    up = _dot(f, wu_ref[...])
    o_ref[...] += _dot((_silu(gate) * up).astype(BF16), wd_ref[...])


def _ffn(h2d, g_ffn, w_gate, w_up, w_down, tm, tf):
    m = h2d.shape[0]
    return pl.pallas_call(
        _ffn_kernel,
        grid=(m // tm, D_FF // tf),
        in_specs=[
            pl.BlockSpec((tm, D_MODEL), lambda i, j: (i, 0)),
            pl.BlockSpec((1, D_MODEL), lambda i, j: (0, 0)),
            pl.BlockSpec((D_MODEL, tf), lambda i, j: (0, j)),
            pl.BlockSpec((D_MODEL, tf), lambda i, j: (0, j)),
            pl.BlockSpec((tf, D_MODEL), lambda i, j: (j, 0)),
        ],
        out_specs=pl.BlockSpec((tm, D_MODEL), lambda i, j: (i, 0)),
        out_shape=jax.ShapeDtypeStruct((m, D_MODEL), F32),
        scratch_shapes=[pltpu.VMEM((tm, D_MODEL), BF16)],
        compiler_params=_cparams("parallel", "arbitrary"),
        name="ffn",
    )(h2d, g_ffn, w_gate, w_up, w_down)


_EXPAND = np.kron(np.eye(SSD_HEADS), np.ones((1, SSD_HEAD_DIM)))


def _prepare(meta_tokens, g_mix, w_in, q_norm, k_norm, rpb, conv_w, conv_b, dt_bias_f, dt_bias_b,
             a_log_f, a_log_b, d_skip, ssd_norm, w_out, g_ffn, w_gate, w_up, w_down):
    l = 0
    p = dict(
        g_mix=g_mix[l].reshape(1, D_MODEL),
        w_in=w_in[l].astype(BF16),
        q_gain=jnp.tile(q_norm[l].astype(F32) * HEAD_DIM ** -0.5, IN_TN // HEAD_DIM).reshape(1, IN_TN),
        k_gain=jnp.tile(k_norm[l].astype(F32), N_KV_HEADS).reshape(1, KV_W),
        ones_bd=jnp.asarray(np.kron(np.eye(MXU_W // HEAD_DIM), np.full((HEAD_DIM, HEAD_DIM), 1.0 / HEAD_DIM)), BF16),
        bias_tab=_bias_table(rpb[l]),
        conv_w=conv_w[l].astype(F32),
        conv_b=conv_b[l].astype(F32).reshape(1, CONV_DIM),
        dt_bias=jnp.concatenate([dt_bias_f[l], dt_bias_b[l]]).astype(F32).reshape(1, 2 * SSD_HEADS),
        a_log=jnp.concatenate([a_log_f[l], a_log_b[l]]).astype(F32).reshape(1, 2 * SSD_HEADS),
        d_skip=jnp.repeat(d_skip[l].astype(F32), SSD_HEAD_DIM).reshape(1, SSD_INNER),
        ssd_norm=ssd_norm[l].astype(F32).reshape(1, SSD_INNER),
        tril=jnp.asarray(np.tril(np.ones((CHUNK, CHUNK))), BF16),
        triu=jnp.asarray(np.triu(np.ones((CHUNK, CHUNK))), BF16),
        expand_f=jnp.asarray(np.concatenate([_EXPAND, 0 * _EXPAND]), BF16),
        expand_b=jnp.asarray(np.concatenate([0 * _EXPAND, _EXPAND]), BF16),
        w_out_a=w_out[l, :ATTN_W].astype(BF16),
        w_out_s=w_out[l, ATTN_W:].astype(BF16),
        g_ffn=g_ffn[l].reshape(1, D_MODEL),
        w_gate=w_gate[l].astype(BF16),
        w_up=w_up[l].astype(BF16),
        w_down=w_down[l].astype(BF16),
    )
    proj_m, dt_m = _in_proj(meta_tokens.astype(F32), p["g_mix"], p["w_in"], p["q_gain"],
                            p["k_gain"], p["ones_bd"], tm=N_META)
    p["k_meta"] = proj_m[:, COL_K:COL_K + KV_W]
    p["v_meta"] = proj_m[:, COL_V:COL_V + KV_W]
    p["proj_meta"] = proj_m
    p["dt_meta"] = dt_m
    return p


def _trunk(x, p):
    b, s, _ = x.shape
    m = b * s
    x2d = x.reshape(m, D_MODEL)
    proj, dt_raw = _in_proj(x2d, p["g_mix"], p["w_in"], p["q_gain"], p["k_gain"], p["ones_bd"], tm=1024)
    proj3 = proj.reshape(b, s, PROJ_W)
    dt3 = dt_raw.reshape(b, s, 2 * SSD_HEADS)

    attn = _attention(proj3, p["bias_tab"], p["k_meta"], p["v_meta"])

    xbc_meta = p["proj_meta"][:, COL_XBC:COL_XBC + CONV_DIM]
    zeros_h = jnp.zeros((b, HALO, CONV_DIM), BF16)
    xc = _conv(proj3, COL_XBC, jnp.broadcast_to(xbc_meta[None], (b, HALO, CONV_DIM)), zeros_h,
               p["conv_w"], p["conv_b"], tc=512, cw=1024, shared=False)
    meta_in = jnp.concatenate([jnp.broadcast_to(xbc_meta[None], (b, N_META, CONV_DIM)),
                               proj3[:, :CHUNK - N_META, COL_XBC:COL_XBC + CONV_DIM]], axis=1)
    xc_meta = _conv(meta_in, 0, zeros_h, zeros_h, p["conv_w"], p["conv_b"], tc=CHUNK, cw=1024, shared=False)

    dt_meta = jnp.broadcast_to(jnp.pad(p["dt_meta"], ((0, CHUNK - N_META), (0, 0)))[None],
                               (b, CHUNK, 2 * SSD_HEADS))
    zero_state = jnp.zeros((b, SSD_GROUPS, SSD_STATE, GROUP_W), F32)
    _, state0 = _ssd_forward(xc_meta, dt_meta, p["dt_bias"], p["a_log"], p["tril"], p["expand_f"],
                             zero_state, n_valid=N_META)
    y_f, _ = _ssd_forward(xc, dt3, p["dt_bias"], p["a_log"], p["tril"], p["expand_f"], state0)
    ssd = _ssd_backward(xc, dt3, p["dt_bias"], p["a_log"], p["triu"], p["expand_b"], y_f, proj3,
                        p["d_skip"], p["ssd_norm"])

    h1 = _out_proj(attn.reshape(m, ATTN_W), ssd.reshape(m, SSD_INNER), p["w_out_a"], p["w_out_s"], x2d,
                   tm=512, tn=1024)
    out = _ffn(h1, p["g_ffn"], p["w_gate"], p["w_up"], p["w_down"], tm=512, tf=512)
    return out.reshape(b, s, D_MODEL)


def kernel(x_prompt, x_sample, meta_tokens, g_mix, w_in, q_norm, k_norm, rpb, conv_w, conv_b, dt_bias_f, dt_bias_b, a_log_f, a_log_b, d_skip, ssd_norm, w_out, g_ffn, w_gate, w_up, w_down):
    p = _prepare(meta_tokens, g_mix, w_in, q_norm, k_norm, rpb, conv_w, conv_b, dt_bias_f, dt_bias_b,
                 a_log_f, a_log_b, d_skip, ssd_norm, w_out, g_ffn, w_gate, w_up, w_down)
    return (_trunk(x_prompt, p), _trunk(x_sample, p))
```

```python
import functools

import numpy as np
import jax
import jax.numpy as jnp
from jax import lax
from jax.experimental import pallas as pl
from jax.experimental.pallas import tpu as pltpu

F32 = jnp.float32
BF16 = jnp.bfloat16

D_MODEL = 2048
N_META = 16
GRID_W = 64
WIN_ROWS = 8
WIN_COLS = 16
N_Q_HEADS = 32
N_KV_HEADS = 8
Q_PER_KV = N_Q_HEADS // N_KV_HEADS
HEAD_DIM = 64
ATTN_W = N_Q_HEADS * HEAD_DIM
KV_W = N_KV_HEADS * HEAD_DIM
SSD_INNER = 2 * D_MODEL
SSD_HEAD_DIM = 64
SSD_HEADS = SSD_INNER // SSD_HEAD_DIM
SSD_GROUPS = 8
SSD_HEADS_PER_GROUP = SSD_HEADS // SSD_GROUPS
SSD_STATE = 128
D_CONV = 4
CHUNK = 128
CONV_DIM = SSD_INNER + 2 * SSD_GROUPS * SSD_STATE
D_FF = 5632
EPS = 1e-6

GROUP_W = SSD_HEADS_PER_GROUP * SSD_HEAD_DIM
PROJ_W = ATTN_W + 2 * KV_W + SSD_INNER + CONV_DIM
COL_Z = 0
COL_XBC = SSD_INNER
COL_Q = COL_XBC + CONV_DIM
COL_K = COL_Q + ATTN_W
COL_V = COL_K + KV_W
NEG = -1e30
VMEM_LIMIT = 56 * 1024 * 1024


def _cparams(*sem):
    return pltpu.CompilerParams(dimension_semantics=sem, vmem_limit_bytes=VMEM_LIMIT)


def _dot(a, b):
    return jnp.dot(a, b, preferred_element_type=F32)


def _dot_nt(a, b):
    return lax.dot_general(a, b, (((1,), (1,)), ((), ())), preferred_element_type=F32)


def _dot_tn(a, b):
    return lax.dot_general(a, b, (((0,), (0,)), ((), ())), preferred_element_type=F32)


def _split2(x):
    hi = x.astype(BF16)
    lo = (x - hi.astype(F32)).astype(BF16)
    return hi, lo


def _split3(x):
    hi = x.astype(BF16)
    r = x - hi.astype(F32)
    mid = r.astype(BF16)
    lo = (r - mid.astype(F32)).astype(BF16)
    return hi, mid, lo


NEG_LOG2_E = -1.4426950408889634


def _silu(x):
    return x / (1.0 + jnp.exp2(x * NEG_LOG2_E))


IN_TN = 1024
N_QK_TILES = ATTN_W // IN_TN
KV_TILE = N_QK_TILES
Z_LAST_TILE = KV_TILE + SSD_INNER // IN_TN


MXU_W = 256


def _head_norm(a, gain, ones_bd):
    hi, lo = _split2(a * a)
    ms = _dot(hi, ones_bd) + _dot(lo, ones_bd)
    return a * lax.rsqrt(ms + EPS) * gain


def _inproj_kernel(x_ref, g_ref, w_ref, wdt_ref, qg_ref, kg_ref, ones_ref, o_ref, dt_ref, u_ref):
    j = pl.program_id(1)

    @pl.when(j == 0)
    def _():
        x = x_ref[...]
        ms = jnp.mean(x * x, axis=-1, keepdims=True)
        u = (x * lax.rsqrt(ms + EPS) * g_ref[...]).astype(BF16)
        u_ref[...] = u
        dt_ref[...] = _dot(u, wdt_ref[...])

    def emit(finish, split):
        u = u_ref[...]
        acc = None if split else _dot(u, w_ref[...])
        for c in range(IN_TN // MXU_W):
            cs = slice(c * MXU_W, (c + 1) * MXU_W)
            a = _dot(u, w_ref[:, cs]) if split else acc[:, cs]
            o_ref[:, cs] = finish(c, cs, a).astype(BF16)

    @pl.when(j < N_QK_TILES)
    def _():
        emit(lambda c, cs, a: _head_norm(a, qg_ref[:, cs], ones_ref[...]), split=False)

    @pl.when(j == KV_TILE)
    def _():
        emit(lambda c, cs, a: _head_norm(a, kg_ref[:, cs], ones_ref[...]) if c < KV_W // MXU_W else a,
             split=False)

    @pl.when((j > KV_TILE) & (j <= Z_LAST_TILE))
    def _():
        emit(lambda c, cs, a: _silu(a), split=True)

    @pl.when(j > Z_LAST_TILE)
    def _():
        emit(lambda c, cs, a: a, split=True)


def _in_proj(x2d, g_mix, w_in, q_gain, k_gain, ones_bd, tm):
    m = x2d.shape[0]
    n_tiles = PROJ_W // IN_TN
    dst = lambda j: jnp.where(j <= KV_TILE, j + (COL_Q // IN_TN), j - (KV_TILE + 1))
    return pl.pallas_call(
        _inproj_kernel,
        grid=(m // tm, n_tiles),
        in_specs=[
            pl.BlockSpec((tm, D_MODEL), lambda i, j: (i, 0)),
            pl.BlockSpec((1, D_MODEL), lambda i, j: (0, 0)),
            pl.BlockSpec((D_MODEL, IN_TN), lambda i, j: (0, j)),
            pl.BlockSpec((D_MODEL, 2 * SSD_HEADS), lambda i, j: (0, PROJ_W // (2 * SSD_HEADS))),
            pl.BlockSpec((1, IN_TN), lambda i, j: (0, 0)),
            pl.BlockSpec((1, KV_W), lambda i, j: (0, 0)),
            pl.BlockSpec((256, 256), lambda i, j: (0, 0)),
        ],
        out_specs=[
            pl.BlockSpec((tm, IN_TN), lambda i, j: (i, dst(j))),
            pl.BlockSpec((tm, 2 * SSD_HEADS), lambda i, j: (i, 0)),
        ],
        out_shape=[
            jax.ShapeDtypeStruct((m, PROJ_W), BF16),
            jax.ShapeDtypeStruct((m, 2 * SSD_HEADS), F32),
        ],
        scratch_shapes=[pltpu.VMEM((tm, D_MODEL), BF16)],
        compiler_params=_cparams("parallel", "arbitrary"),
        name="in_proj",
    )(x2d, g_mix, w_in, w_in, q_gain, k_gain, ones_bd)


N_BAND = WIN_ROWS * GRID_W
HALF_W = GRID_W // 2
ATTN_ROWS = 4
META_AT = {0: GRID_W - N_META, 1: 0}


def _swap_halves(t):
    return jnp.concatenate([t[:, HEAD_DIM:], t[:, :HEAD_DIM]], axis=1)


def _with_meta(band, meta, hf):
    at = META_AT[hf]
    parts = [band[:at]] if at else []
    return jnp.concatenate(parts + [meta, band[at + N_META:]], axis=0)


def _attn_kernel(q_ref, k_ref, v_ref, bias_ref, km_ref, vm_ref, o_ref, *, rows):
    n_rows = q_ref.shape[1] // GRID_W
    r = [pl.program_id(1) * n_rows + ri for ri in range(n_rows)]
    rs = [jnp.clip(x - WIN_ROWS // 2, 0, rows - WIN_ROWS) for x in r]
    start = [pl.multiple_of(x * GRID_W, GRID_W) for x in rs]
    d0 = [WIN_ROWS - 1 - (x - y) for x, y in zip(r, rs)]
    lo = lax.broadcasted_iota(jnp.int32, (HALF_W, 2 * HEAD_DIM), 1) < HEAD_DIM
    hi = jnp.logical_not(lo)
    ones = jnp.ones((N_BAND, 2 * HEAD_DIM), BF16)
    units = [(ri, kp, hf) for ri in range(n_rows) for kp in range(N_KV_HEADS // 2) for hf in range(2)]

    def scores(ri, kp, hf):
        ps = slice(kp * 2 * HEAD_DIM, (kp + 1) * 2 * HEAD_DIM)
        kb = _with_meta(k_ref[0, pl.ds(start[ri], N_BAND), ps], km_ref[:, ps], hf)
        q0 = ri * GRID_W + hf * HALF_W
        tiles = []
        for par in range(2):
            keep = lo if par == 0 else hi
            for t in range(Q_PER_KV // 2):
                c0 = ((2 * kp + par) * 2 + t) * 128
                qt = q_ref[0, q0:q0 + HALF_W, c0:c0 + 128]
                qs = _swap_halves(qt)
                for piece in ((qt, qs) if par == 0 else (qs, qt)):
                    tiles.append(jnp.where(keep, piece, jnp.zeros_like(piece)))
        return _dot_nt(jnp.concatenate(tiles, axis=0), kb)

    lane = lax.broadcasted_iota(jnp.int32, (1, 2 * HEAD_DIM), 1)
    s_next = scores(*units[0])
    for idx, (ri, kp, hf) in enumerate(units):
        s_raw = s_next
        if idx + 1 < len(units):
            s_next = scores(*units[idx + 1])
        ps = slice(kp * 2 * HEAD_DIM, (kp + 1) * 2 * HEAD_DIM)
        vb = _with_meta(v_ref[0, pl.ds(start[ri], N_BAND), ps], vm_ref[:, ps], hf)
        q0 = ri * GRID_W + hf * HALF_W
        s = []
        for j in range(WIN_ROWS // 2):
            bias = bias_ref[kp, hf, d0[ri] + 2 * j]
            if j == 0:
                at = META_AT[hf]
                bias = jnp.where((lane >= at) & (lane < at + N_META), 0.0, bias)
            s.append(s_raw[:, j * 128:(j + 1) * 128] + bias)
        m = jnp.max(jnp.maximum(jnp.maximum(s[0], s[1]), jnp.maximum(s[2], s[3])), axis=1, keepdims=True)
        p = jnp.concatenate([jnp.exp(sj - m) for sj in s], axis=1).astype(BF16)
        oa = _dot(p, jnp.concatenate([vb, ones], axis=1))
        o = oa[:, :128] / oa[:, 128:]
        for par in range(2):
            for t in range(Q_PER_KV // 2):
                base = (par * Q_PER_KV + 2 * t) * HALF_W
                a = o[base:base + HALF_W]
                b = o[base + HALF_W:base + 2 * HALF_W]
                if par == 0:
                    tile = jnp.where(lo, a, pltpu.roll(b, HEAD_DIM, 1))
                else:
                    tile = jnp.where(lo, pltpu.roll(a, HEAD_DIM, 1), b)
                c0 = ((2 * kp + par) * 2 + t) * 128
                o_ref[0, q0:q0 + HALF_W, c0:c0 + 128] = tile.astype(BF16)


def _attention(proj3, bias_tab, k_meta, v_meta):
    b, s, _ = proj3.shape
    rows = s // GRID_W
    return pl.pallas_call(
        functools.partial(_attn_kernel, rows=rows),
        grid=(b, rows // ATTN_ROWS),
        in_specs=[
            pl.BlockSpec((1, ATTN_ROWS * GRID_W, ATTN_W), lambda i, r: (i, r, COL_Q // ATTN_W)),
            pl.BlockSpec((1, s, KV_W), lambda i, r: (i, 0, COL_K // KV_W)),
            pl.BlockSpec((1, s, KV_W), lambda i, r: (i, 0, COL_V // KV_W)),
            pl.BlockSpec(bias_tab.shape, lambda i, r: (0, 0, 0, 0, 0), pipeline_mode=pl.Buffered(1)),
            pl.BlockSpec((N_META, KV_W), lambda i, r: (0, 0)),
            pl.BlockSpec((N_META, KV_W), lambda i, r: (0, 0)),
        ],
        out_specs=pl.BlockSpec((1, ATTN_ROWS * GRID_W, ATTN_W), lambda i, r: (i, r, 0)),
        out_shape=jax.ShapeDtypeStruct((b, s, ATTN_W), BF16),
        compiler_params=_cparams("parallel", "arbitrary"),
        name="na_attention",
    )(proj3, proj3, proj3, bias_tab, k_meta, v_meta)


def _bias_table(rpb):
    c = np.arange(GRID_W)
    cs = np.clip(c - WIN_COLS // 2, 0, GRID_W - WIN_COLS)
    inwin = (c[None, :] >= cs[:, None]) & (c[None, :] < cs[:, None] + WIN_COLS)
    dc = np.clip(c[None, :] - c[:, None] + WIN_COLS - 1, 0, 2 * WIN_COLS - 2)
    pick = jnp.asarray(dc[None] == np.arange(2 * WIN_COLS - 1)[:, None, None], F32)
    slab = jnp.einsum("hdk,kcj->hdcj", rpb.astype(F32), pick, precision=lax.Precision.HIGHEST)
    slab = jnp.where(inwin[None, None], slab, NEG)
    nd = 2 * WIN_ROWS - 1
    slab = slab.reshape(N_KV_HEADS // 2, 2 * Q_PER_KV, nd, 2, HALF_W, GRID_W)
    slab = slab.transpose(0, 3, 2, 1, 4, 5).reshape(N_KV_HEADS // 2, 2, nd, 2 * Q_PER_KV * HALF_W, GRID_W)
    return jnp.concatenate([slab[:, :, :nd - 1], slab[:, :, 1:]], axis=-1)


HALO = 16
CONV_SUB = 128
CONV_K = 256
SHIFT_TAPS = (0, 1, 3)


def _shift_matrix():
    m = np.zeros((len(SHIFT_TAPS) * CONV_SUB, CONV_K), np.float32)
    for k, tap in enumerate(SHIFT_TAPS):
        t = np.arange(CONV_SUB)
        m[k * CONV_SUB + t, t + HALO + tap - D_CONV // 2] = 1.0
    return m


def _conv_kernel(cur_ref, prev_ref, next_ref, lh_ref, rh_ref, w_ref, b_ref, t_ref, o_ref):
    i = pl.program_id(1)
    last = pl.num_programs(1) - 1
    tc, cw = cur_ref.shape[1], cur_ref.shape[2]
    before = jnp.where(i == 0, lh_ref[0], prev_ref[0])
    after = jnp.where(i == last, rh_ref[0], next_ref[0])
    pad = jnp.zeros((CONV_K - CONV_SUB - 2 * HALO, cw), BF16)
    ext = jnp.concatenate([before, cur_ref[0], after, pad], axis=0)
    w = w_ref[...]
    shift = t_ref[...]
    for blk in range(tc // CONV_SUB):
        r0 = blk * CONV_SUB
        sh = _dot(shift, ext[r0:r0 + CONV_K])
        y = b_ref[...] + w[D_CONV // 2:D_CONV // 2 + 1] * cur_ref[0, r0:r0 + CONV_SUB].astype(F32)
        for k, tap in enumerate(SHIFT_TAPS):
            y = y + w[tap:tap + 1] * sh[k * CONV_SUB:(k + 1) * CONV_SUB]
        o_ref[0, r0:r0 + CONV_SUB] = _silu(y).astype(BF16)


def _conv(src, col0, lh, rh, conv_w, conv_b, tc, cw, shared):
    b = lh.shape[0]
    s = src.shape[1]
    hb = tc // HALO
    nhb = s // HALO
    c0 = col0 // cw
    shift = jnp.asarray(_shift_matrix(), BF16)
    bi = (lambda i: 0) if shared else (lambda i: i)
    return pl.pallas_call(
        _conv_kernel,
        grid=(b, s // tc, CONV_DIM // cw),
        in_specs=[
            pl.BlockSpec((1, tc, cw), lambda i, t, c: (bi(i), t, c0 + c)),
            pl.BlockSpec((1, HALO, cw), lambda i, t, c: (bi(i), jnp.maximum(t * hb - 1, 0), c0 + c)),
            pl.BlockSpec((1, HALO, cw), lambda i, t, c: (bi(i), jnp.minimum((t + 1) * hb, nhb - 1), c0 + c)),
            pl.BlockSpec((1, HALO, cw), lambda i, t, c: (i, 0, c)),
            pl.BlockSpec((1, HALO, cw), lambda i, t, c: (i, 0, c)),
            pl.BlockSpec((D_CONV, cw), lambda i, t, c: (0, c)),
            pl.BlockSpec((1, cw), lambda i, t, c: (0, c)),
            pl.BlockSpec(shift.shape, lambda i, t, c: (0, 0)),
        ],
        out_specs=pl.BlockSpec((1, tc, cw), lambda i, t, c: (i, t, c)),
        out_shape=jax.ShapeDtypeStruct((b, s, CONV_DIM), BF16),
        compiler_params=_cparams("parallel", "arbitrary", "arbitrary"),
        name="conv_silu",
    )(src, src, src, lh, rh, conv_w, conv_b, shift)


def _softplus(x):
    return jnp.maximum(x, 0.0) + jnp.log1p(jnp.exp(-jnp.abs(x)))


def _ssd_chunks(x_ref, b_ref, c_ref, dtraw_ref, dtb_ref, alog_ref, tri_ref, exp_ref, state_ref, emit,
                *, reverse, n_valid):
    n_seq, L = x_ref.shape[0], x_ref.shape[1]
    off = SSD_HEADS if reverse else 0
    li = lax.broadcasted_iota(jnp.int32, (L, L), 0)
    si = lax.broadcasted_iota(jnp.int32, (L, L), 1)
    keep = (si >= li) if reverse else (si <= li)
    lane = lax.broadcasted_iota(jnp.int32, (L, 2 * SSD_HEAD_DIM), 1)
    left = lane < SSD_HEAD_DIM
    tri = tri_ref[...]

    def per_head(bi):
        dt = _softplus(dtraw_ref[bi] + dtb_ref[...])
        if n_valid < L:
            dt = jnp.where(lax.broadcasted_iota(jnp.int32, dt.shape, 0) < n_valid, dt, 0.0)
        da = dt * (-jnp.exp(alog_ref[...]))
        acs = sum(_dot(tri, piece) for piece in _split3(da))
        tot = acs[0:1] if reverse else acs[L - 1:L]
        src_t = (acs - jnp.log(dt)).T
        decay_out = jnp.exp(acs).astype(BF16)
        decay_in = (jnp.exp(tot - acs) * dt).astype(BF16)
        return acs, src_t, decay_out, decay_in

    heads = [per_head(bi) for bi in range(n_seq)]
    for g in range(SSD_GROUPS):
        gs = slice(g * GROUP_W, (g + 1) * GROUP_W)
        ns = slice(g * SSD_STATE, (g + 1) * SSD_STATE)
        expand = exp_ref[:, gs]
        for bi in range(n_seq):
            acs, src_t, decay_out, decay_in = heads[bi]
            e_out = _dot(decay_out, expand)
            e_tot = e_out[0:1] if reverse else e_out[L - 1:L]
            xw = (x_ref[bi, :, gs].astype(F32) * _dot(decay_in, expand)).astype(BF16)
            bg = b_ref[bi, :, ns]
            cg = c_ref[bi, :, ns]
            cb = _dot_nt(cg, bg)
            st = state_ref[bi, g]
            y_inter = _dot(cg, st.astype(BF16)) * e_out
            state_ref[bi, g] = st * e_tot + _dot_tn(bg, xw)
            pairs = []
            for j in range(SSD_HEADS_PER_GROUP // 2):
                ms = []
                for h in (off + g * SSD_HEADS_PER_GROUP + 2 * j, off + g * SSD_HEADS_PER_GROUP + 2 * j + 1):
                    seg = acs[:, h:h + 1] - src_t[h:h + 1, :]
                    ms.append((cb * jnp.exp(jnp.where(keep, seg, -jnp.inf))).astype(BF16))
                xp = x_ref[bi, :, g * GROUP_W + j * 128:g * GROUP_W + (j + 1) * 128]
                zero = jnp.zeros_like(xp)
                xbd = jnp.concatenate([jnp.where(left, xp, zero), jnp.where(left, zero, xp)], axis=0)
                pairs.append(_dot(jnp.concatenate(ms, axis=1), xbd))
            emit(bi, gs, jnp.concatenate(pairs, axis=1) + y_inter)


def _ssd_fwd_kernel(x_ref, b_ref, c_ref, dtraw_ref, dtb_ref, alog_ref, tri_ref, exp_ref, init_ref,
                    y_ref, fin_ref, state_ref, *, n_valid):
    c = pl.program_id(1)

    @pl.when(c == 0)
    def _():
        state_ref[...] = init_ref[...]

    def emit(bi, gs, y):
        y_ref[bi, :, gs] = y

    _ssd_chunks(x_ref, b_ref, c_ref, dtraw_ref, dtb_ref, alog_ref, tri_ref, exp_ref, state_ref, emit,
                reverse=False, n_valid=n_valid)

    @pl.when(c == pl.num_programs(1) - 1)
    def _():
        fin_ref[...] = state_ref[...]


def _ssd_bwd_kernel(x_ref, b_ref, c_ref, dtraw_ref, dtb_ref, alog_ref, tri_ref, exp_ref,
                    yf_ref, z_ref, dskip_ref, nw_ref, o_ref, state_ref):
    c = pl.program_id(1)

    @pl.when(c == 0)
    def _():
        state_ref[...] = jnp.zeros_like(state_ref)

    def emit(bi, gs, y):
        y = y + yf_ref[bi, :, gs] + x_ref[bi, :, gs].astype(F32) * dskip_ref[:, gs]
        y = y * z_ref[bi, :, gs].astype(F32)
        ms = jnp.mean(y * y, axis=-1, keepdims=True)
        o_ref[bi, :, gs] = (y * lax.rsqrt(ms + EPS) * nw_ref[:, gs]).astype(BF16)

    _ssd_chunks(x_ref, b_ref, c_ref, dtraw_ref, dtb_ref, alog_ref, tri_ref, exp_ref, state_ref, emit,
                reverse=True, n_valid=x_ref.shape[1])


def _seqs_per_step(b):
    return 2 if b % 2 == 0 else 1


def _ssd_common_specs(ns, L, cmap):
    nb = SSD_GROUPS * SSD_STATE
    return [
        pl.BlockSpec((ns, L, SSD_INNER), lambda i, c: (i, cmap(c), 0)),
        pl.BlockSpec((ns, L, nb), lambda i, c: (i, cmap(c), SSD_INNER // nb)),
        pl.BlockSpec((ns, L, nb), lambda i, c: (i, cmap(c), SSD_INNER // nb + 1)),
    ]


def _const_spec(shape):
    nd = len(shape)
    return pl.BlockSpec(shape, lambda i, c: (0,) * nd)


def _ssd_forward(xc, dt_raw, dt_bias, a_log, tri, expand, init, n_valid=CHUNK):
    b, s, _ = xc.shape
    L = CHUNK
    ns = _seqs_per_step(b)
    st_shape = (SSD_GROUPS, SSD_STATE, GROUP_W)
    nc = s // L
    return pl.pallas_call(
        functools.partial(_ssd_fwd_kernel, n_valid=n_valid),
        grid=(b // ns, nc),
        in_specs=_ssd_common_specs(ns, L, lambda c: c) + [
            pl.BlockSpec((ns, L, 2 * SSD_HEADS), lambda i, c: (i, c, 0)),
            _const_spec((1, 2 * SSD_HEADS)), _const_spec((1, 2 * SSD_HEADS)),
            _const_spec((L, L)), _const_spec((2 * SSD_HEADS, SSD_INNER)),
            pl.BlockSpec((ns,) + st_shape, lambda i, c: (i, 0, 0, 0)),
        ],
        out_specs=[
            pl.BlockSpec((ns, L, SSD_INNER), lambda i, c: (i, c, 0)),
            pl.BlockSpec((ns,) + st_shape, lambda i, c: (i, 0, 0, 0)),
        ],
        out_shape=[
            jax.ShapeDtypeStruct((b, s, SSD_INNER), F32),
            jax.ShapeDtypeStruct((b,) + st_shape, F32),
        ],
        scratch_shapes=[pltpu.VMEM((ns,) + st_shape, F32)],
        compiler_params=_cparams("parallel", "arbitrary"),
        name="ssd_forward",
    )(xc, xc, xc, dt_raw, dt_bias, a_log, tri, expand, init)


def _ssd_backward(xc, dt_raw, dt_bias, a_log, tri, expand, y_f, proj3, d_skip, norm_w):
    b, s, _ = xc.shape
    L = CHUNK
    nc = s // L
    ns = _seqs_per_step(b)
    rev = lambda c: nc - 1 - c
    return pl.pallas_call(
        _ssd_bwd_kernel,
        grid=(b // ns, nc),
        in_specs=_ssd_common_specs(ns, L, rev) + [
            pl.BlockSpec((ns, L, 2 * SSD_HEADS), lambda i, c: (i, rev(c), 0)),
            _const_spec((1, 2 * SSD_HEADS)), _const_spec((1, 2 * SSD_HEADS)),
            _const_spec((L, L)), _const_spec((2 * SSD_HEADS, SSD_INNER)),
            pl.BlockSpec((ns, L, SSD_INNER), lambda i, c: (i, rev(c), 0)),
            pl.BlockSpec((ns, L, SSD_INNER), lambda i, c: (i, rev(c), COL_Z // SSD_INNER)),
            _const_spec((1, SSD_INNER)), _const_spec((1, SSD_INNER)),
        ],
        out_specs=pl.BlockSpec((ns, L, SSD_INNER), lambda i, c: (i, rev(c), 0)),
        out_shape=jax.ShapeDtypeStruct((b, s, SSD_INNER), BF16),
        scratch_shapes=[pltpu.VMEM((ns, SSD_GROUPS, SSD_STATE, GROUP_W), F32)],
        compiler_params=_cparams("parallel", "arbitrary"),
        name="ssd_backward",
    )(xc, xc, xc, dt_raw, dt_bias, a_log, tri, expand, y_f, proj3, d_skip, norm_w)


def _outproj_kernel(a_ref, s_ref, wa_ref, ws_ref, x_ref, o_ref):
    o_ref[...] = x_ref[...] + _dot(a_ref[...], wa_ref[...]) + _dot(s_ref[...], ws_ref[...])


def _out_proj(attn2, ssd2, w_a, w_s, x2d, tm, tn):
    m = x2d.shape[0]
    return pl.pallas_call(
        _outproj_kernel,
        grid=(D_MODEL // tn, m // tm),
        in_specs=[
            pl.BlockSpec((tm, ATTN_W), lambda j, i: (i, 0)),
            pl.BlockSpec((tm, SSD_INNER), lambda j, i: (i, 0)),
            pl.BlockSpec((ATTN_W, tn), lambda j, i: (0, j)),
            pl.BlockSpec((SSD_INNER, tn), lambda j, i: (0, j)),
            pl.BlockSpec((tm, tn), lambda j, i: (i, j)),
        ],
        out_specs=pl.BlockSpec((tm, tn), lambda j, i: (i, j)),
        out_shape=jax.ShapeDtypeStruct((m, D_MODEL), F32),
        compiler_params=_cparams("parallel", "parallel"),
        name="out_proj",
    )(attn2, ssd2, w_a, w_s, x2d)


def _ffn_kernel(h_ref, g_ref, wg_ref, wu_ref, wd_ref, o_ref, f_ref):
    j = pl.program_id(1)

    @pl.when(j == 0)
    def _():
        h = h_ref[...]
        ms = jnp.mean(h * h, axis=-1, keepdims=True)
        f_ref[...] = (h * lax.rsqrt(ms + EPS) * g_ref[...]).astype(BF16)
        o_ref[...] = h

    f = f_ref[...]
    gate = _dot(f, wg_ref[...])
    up = _dot(f, wu_ref[...])
    o_ref[...] += _dot((_silu(gate) * up).astype(BF16), wd_ref[...])


def _ffn(h2d, g_ffn, w_gate, w_up, w_down, tm, tf):
    m = h2d.shape[0]
    return pl.pallas_call(
        _ffn_kernel,
        grid=(m // tm, D_FF // tf),
        in_specs=[
            pl.BlockSpec((tm, D_MODEL), lambda i, j: (i, 0)),
            pl.BlockSpec((1, D_MODEL), lambda i, j: (0, 0)),
            pl.BlockSpec((D_MODEL, tf), lambda i, j: (0, j)),
            pl.BlockSpec((D_MODEL, tf), lambda i, j: (0, j)),
            pl.BlockSpec((tf, D_MODEL), lambda i, j: (j, 0)),
        ],
        out_specs=pl.BlockSpec((tm, D_MODEL), lambda i, j: (i, 0)),
        out_shape=jax.ShapeDtypeStruct((m, D_MODEL), F32),
        scratch_shapes=[pltpu.VMEM((tm, D_MODEL), BF16)],
        compiler_params=_cparams("parallel", "arbitrary"),
        name="ffn",
    )(h2d, g_ffn, w_gate, w_up, w_down)


_EXPAND = np.kron(np.eye(SSD_HEADS), np.ones((1, SSD_HEAD_DIM)))


def _prepare(meta_tokens, g_mix, w_in, q_norm, k_norm, rpb, conv_w, conv_b, dt_bias_f, dt_bias_b,
             a_log_f, a_log_b, d_skip, ssd_norm, w_out, g_ffn, w_gate, w_up, w_down):
    l = 0
    p = dict(
        g_mix=g_mix[l].reshape(1, D_MODEL),
        w_in=w_in[l].astype(BF16),
        q_gain=jnp.tile(q_norm[l].astype(F32) * HEAD_DIM ** -0.5, IN_TN // HEAD_DIM).reshape(1, IN_TN),
        k_gain=jnp.tile(k_norm[l].astype(F32), N_KV_HEADS).reshape(1, KV_W),
        ones_bd=jnp.asarray(np.kron(np.eye(MXU_W // HEAD_DIM), np.full((HEAD_DIM, HEAD_DIM), 1.0 / HEAD_DIM)), BF16),
        bias_tab=_bias_table(rpb[l]),
        conv_w=conv_w[l].astype(F32),
        conv_b=conv_b[l].astype(F32).reshape(1, CONV_DIM),
        dt_bias=jnp.concatenate([dt_bias_f[l], dt_bias_b[l]]).astype(F32).reshape(1, 2 * SSD_HEADS),
        a_log=jnp.concatenate([a_log_f[l], a_log_b[l]]).astype(F32).reshape(1, 2 * SSD_HEADS),
        d_skip=jnp.repeat(d_skip[l].astype(F32), SSD_HEAD_DIM).reshape(1, SSD_INNER),
        ssd_norm=ssd_norm[l].astype(F32).reshape(1, SSD_INNER),
        tril=jnp.asarray(np.tril(np.ones((CHUNK, CHUNK))), BF16),
        triu=jnp.asarray(np.triu(np.ones((CHUNK, CHUNK))), BF16),
        expand_f=jnp.asarray(np.concatenate([_EXPAND, 0 * _EXPAND]), BF16),
        expand_b=jnp.asarray(np.concatenate([0 * _EXPAND, _EXPAND]), BF16),
        w_out_a=w_out[l, :ATTN_W].astype(BF16),
        w_out_s=w_out[l, ATTN_W:].astype(BF16),
        g_ffn=g_ffn[l].reshape(1, D_MODEL),
        w_gate=w_gate[l].astype(BF16),
        w_up=w_up[l].astype(BF16),
        w_down=w_down[l].astype(BF16),
    )
    proj_m, dt_m = _in_proj(meta_tokens.astype(F32), p["g_mix"], p["w_in"], p["q_gain"],
                            p["k_gain"], p["ones_bd"], tm=N_META)
    p["k_meta"] = proj_m[:, COL_K:COL_K + KV_W]
    p["v_meta"] = proj_m[:, COL_V:COL_V + KV_W]
    p["proj_meta"] = proj_m
    p["dt_meta"] = dt_m
    return p


def _trunk(x, p):
    b, s, _ = x.shape
    m = b * s
    x2d = x.reshape(m, D_MODEL)
    proj, dt_raw = _in_proj(x2d, p["g_mix"], p["w_in"], p["q_gain"], p["k_gain"], p["ones_bd"], tm=1024)
    proj3 = proj.reshape(b, s, PROJ_W)
    dt3 = dt_raw.reshape(b, s, 2 * SSD_HEADS)

    attn = _attention(proj3, p["bias_tab"], p["k_meta"], p["v_meta"])

    xbc_meta = p["proj_meta"][:, COL_XBC:COL_XBC + CONV_DIM]
    zeros_h = jnp.zeros((b, HALO, CONV_DIM), BF16)
    xc = _conv(proj3, COL_XBC, jnp.broadcast_to(xbc_meta[None], (b, HALO, CONV_DIM)), zeros_h,
               p["conv_w"], p["conv_b"], tc=min(s, 2048), cw=2048, shared=False)
    meta_in = jnp.concatenate([jnp.broadcast_to(xbc_meta[None], (b, N_META, CONV_DIM)),
                               proj3[:, :CHUNK - N_META, COL_XBC:COL_XBC + CONV_DIM]], axis=1)
    xc_meta = _conv(meta_in, 0, zeros_h, zeros_h, p["conv_w"], p["conv_b"], tc=CHUNK, cw=2048, shared=False)

    dt_meta = jnp.broadcast_to(jnp.pad(p["dt_meta"], ((0, CHUNK - N_META), (0, 0)))[None],
                               (b, CHUNK, 2 * SSD_HEADS))
    zero_state = jnp.zeros((b, SSD_GROUPS, SSD_STATE, GROUP_W), F32)
    _, state0 = _ssd_forward(xc_meta, dt_meta, p["dt_bias"], p["a_log"], p["tril"], p["expand_f"],
                             zero_state, n_valid=N_META)
    y_f, _ = _ssd_forward(xc, dt3, p["dt_bias"], p["a_log"], p["tril"], p["expand_f"], state0)
    ssd = _ssd_backward(xc, dt3, p["dt_bias"], p["a_log"], p["triu"], p["expand_b"], y_f, proj3,
                        p["d_skip"], p["ssd_norm"])

    h1 = _out_proj(attn.reshape(m, ATTN_W), ssd.reshape(m, SSD_INNER), p["w_out_a"], p["w_out_s"], x2d,
                   tm=512, tn=1024)
    out = _ffn(h1, p["g_ffn"], p["w_gate"], p["w_up"], p["w_down"], tm=1024, tf=512)
    return out.reshape(b, s, D_MODEL)


def kernel(x_prompt, x_sample, meta_tokens, g_mix, w_in, q_norm, k_norm, rpb, conv_w, conv_b, dt_bias_f, dt_bias_b, a_log_f, a_log_b, d_skip, ssd_norm, w_out, g_ffn, w_gate, w_up, w_down):
    p = _prepare(meta_tokens, g_mix, w_in, q_norm, k_norm, rpb, conv_w, conv_b, dt_bias_f, dt_bias_b,
                 a_log_f, a_log_b, d_skip, ssd_norm, w_out, g_ffn, w_gate, w_up, w_down)
    return (_trunk(x_prompt, p), _trunk(x_sample, p))
```

```python
import functools

import numpy as np
import jax
import jax.numpy as jnp
from jax import lax
from jax.experimental import pallas as pl
from jax.experimental.pallas import tpu as pltpu

F32 = jnp.float32
BF16 = jnp.bfloat16

D_MODEL = 2048
N_META = 16
GRID_W = 64
WIN_ROWS = 8
WIN_COLS = 16
N_Q_HEADS = 32
N_KV_HEADS = 8
Q_PER_KV = N_Q_HEADS // N_KV_HEADS
HEAD_DIM = 64
ATTN_W = N_Q_HEADS * HEAD_DIM
KV_W = N_KV_HEADS * HEAD_DIM
SSD_INNER = 2 * D_MODEL
SSD_HEAD_DIM = 64
SSD_HEADS = SSD_INNER // SSD_HEAD_DIM
SSD_GROUPS = 8
SSD_HEADS_PER_GROUP = SSD_HEADS // SSD_GROUPS
SSD_STATE = 128
D_CONV = 4
CHUNK = 128
CONV_DIM = SSD_INNER + 2 * SSD_GROUPS * SSD_STATE
D_FF = 5632
EPS = 1e-6

GROUP_W = SSD_HEADS_PER_GROUP * SSD_HEAD_DIM
PROJ_W = ATTN_W + 2 * KV_W + SSD_INNER + CONV_DIM
COL_Z = 0
COL_XBC = SSD_INNER
COL_Q = COL_XBC + CONV_DIM
COL_K = COL_Q + ATTN_W
COL_V = COL_K + KV_W
NEG = -1e30
VMEM_LIMIT = 56 * 1024 * 1024


def _cparams(*sem):
    return pltpu.CompilerParams(dimension_semantics=sem, vmem_limit_bytes=VMEM_LIMIT)


def _dot(a, b):
    return jnp.dot(a, b, preferred_element_type=F32)


def _dot_nt(a, b):
    return lax.dot_general(a, b, (((1,), (1,)), ((), ())), preferred_element_type=F32)


def _dot_tn(a, b):
    return lax.dot_general(a, b, (((0,), (0,)), ((), ())), preferred_element_type=F32)


def _split2(x):
    hi = x.astype(BF16)
    lo = (x - hi.astype(F32)).astype(BF16)
    return hi, lo


def _split3(x):
    hi = x.astype(BF16)
    r = x - hi.astype(F32)
    mid = r.astype(BF16)
    lo = (r - mid.astype(F32)).astype(BF16)
    return hi, mid, lo


NEG_LOG2_E = -1.4426950408889634


def _silu(x):
    return x / (1.0 + jnp.exp2(x * NEG_LOG2_E))


IN_TN = 1024
N_QK_TILES = ATTN_W // IN_TN
KV_TILE = N_QK_TILES
Z_LAST_TILE = KV_TILE + SSD_INNER // IN_TN


MXU_W = 256


def _head_norm(a, gain, ones_bd):
    hi, lo = _split2(a * a)
    ms = _dot(hi, ones_bd) + _dot(lo, ones_bd)
    return a * lax.rsqrt(ms + EPS) * gain


def _inproj_kernel(x_ref, g_ref, w_ref, wdt_ref, qg_ref, kg_ref, ones_ref, o_ref, dt_ref, u_ref):
    j = pl.program_id(1)

    @pl.when(j == 0)
    def _():
        x = x_ref[...]
        ms = jnp.mean(x * x, axis=-1, keepdims=True)
        u = (x * lax.rsqrt(ms + EPS) * g_ref[...]).astype(BF16)
        u_ref[...] = u
        dt_ref[...] = _dot(u, wdt_ref[...])

    def emit(finish, split):
        u = u_ref[...]
        acc = None if split else _dot(u, w_ref[...])
        for c in range(IN_TN // MXU_W):
            cs = slice(c * MXU_W, (c + 1) * MXU_W)
            a = _dot(u, w_ref[:, cs]) if split else acc[:, cs]
            o_ref[:, cs] = finish(c, cs, a).astype(BF16)

    @pl.when(j < N_QK_TILES)
    def _():
        emit(lambda c, cs, a: _head_norm(a, qg_ref[:, cs], ones_ref[...]), split=False)

    @pl.when(j == KV_TILE)
    def _():
        emit(lambda c, cs, a: _head_norm(a, kg_ref[:, cs], ones_ref[...]) if c < KV_W // MXU_W else a,
             split=False)

    @pl.when((j > KV_TILE) & (j <= Z_LAST_TILE))
    def _():
        emit(lambda c, cs, a: _silu(a), split=True)

    @pl.when(j > Z_LAST_TILE)
    def _():
        emit(lambda c, cs, a: a, split=True)


def _in_proj(x2d, g_mix, w_in, q_gain, k_gain, ones_bd, tm):
    m = x2d.shape[0]
    n_tiles = PROJ_W // IN_TN
    dst = lambda j: jnp.where(j <= KV_TILE, j + (COL_Q // IN_TN), j - (KV_TILE + 1))
    return pl.pallas_call(
        _inproj_kernel,
        grid=(m // tm, n_tiles),
        in_specs=[
            pl.BlockSpec((tm, D_MODEL), lambda i, j: (i, 0)),
            pl.BlockSpec((1, D_MODEL), lambda i, j: (0, 0)),
            pl.BlockSpec((D_MODEL, IN_TN), lambda i, j: (0, j)),
            pl.BlockSpec((D_MODEL, 2 * SSD_HEADS), lambda i, j: (0, PROJ_W // (2 * SSD_HEADS))),
            pl.BlockSpec((1, IN_TN), lambda i, j: (0, 0)),
            pl.BlockSpec((1, KV_W), lambda i, j: (0, 0)),
            pl.BlockSpec((256, 256), lambda i, j: (0, 0)),
        ],
        out_specs=[
            pl.BlockSpec((tm, IN_TN), lambda i, j: (i, dst(j))),
            pl.BlockSpec((tm, 2 * SSD_HEADS), lambda i, j: (i, 0)),
        ],
        out_shape=[
            jax.ShapeDtypeStruct((m, PROJ_W), BF16),
            jax.ShapeDtypeStruct((m, 2 * SSD_HEADS), F32),
        ],
        scratch_shapes=[pltpu.VMEM((tm, D_MODEL), BF16)],
        compiler_params=_cparams("parallel", "arbitrary"),
        name="in_proj",
    )(x2d, g_mix, w_in, w_in, q_gain, k_gain, ones_bd)


N_BAND = WIN_ROWS * GRID_W
HALF_W = GRID_W // 2
ATTN_ROWS = 4
META_AT = {0: GRID_W - N_META, 1: 0}


def _swap_halves(t):
    return jnp.concatenate([t[:, HEAD_DIM:], t[:, :HEAD_DIM]], axis=1)


def _with_meta(band, meta, hf):
    at = META_AT[hf]
    parts = [band[:at]] if at else []
    return jnp.concatenate(parts + [meta, band[at + N_META:]], axis=0)


def _attn_kernel(q_ref, kv_ref, bias_ref, km_ref, vm_ref, o_ref, *, rows):
    n_rows = q_ref.shape[1] // GRID_W
    r = [pl.program_id(1) * n_rows + ri for ri in range(n_rows)]
    rs = [jnp.clip(x - WIN_ROWS // 2, 0, rows - WIN_ROWS) for x in r]
    start = [pl.multiple_of(x * GRID_W, GRID_W) for x in rs]
    d0 = [WIN_ROWS - 1 - (x - y) for x, y in zip(r, rs)]
    lo = lax.broadcasted_iota(jnp.int32, (HALF_W, 2 * HEAD_DIM), 1) < HEAD_DIM
    hi = jnp.logical_not(lo)
    ones = jnp.ones((N_BAND, 2 * HEAD_DIM), BF16)
    units = [(ri, kp, hf) for ri in range(n_rows) for kp in range(N_KV_HEADS // 2) for hf in range(2)]

    def scores(ri, kp, hf):
        ps = slice(kp * 2 * HEAD_DIM, (kp + 1) * 2 * HEAD_DIM)
        kb = _with_meta(kv_ref[0, pl.ds(start[ri], N_BAND), ps], km_ref[:, ps], hf)
        q0 = ri * GRID_W + hf * HALF_W
        tiles = []
        for par in range(2):
            keep = lo if par == 0 else hi
            for t in range(Q_PER_KV // 2):
                c0 = ((2 * kp + par) * 2 + t) * 128
                qt = q_ref[0, q0:q0 + HALF_W, c0:c0 + 128]
                qs = _swap_halves(qt)
                for piece in ((qt, qs) if par == 0 else (qs, qt)):
                    tiles.append(jnp.where(keep, piece, jnp.zeros_like(piece)))
        return _dot_nt(jnp.concatenate(tiles, axis=0), kb)

    lane = lax.broadcasted_iota(jnp.int32, (1, 2 * HEAD_DIM), 1)
    s_next = scores(*units[0])
    for idx, (ri, kp, hf) in enumerate(units):
        s_raw = s_next
        if idx + 1 < len(units):
            s_next = scores(*units[idx + 1])
        ps = slice(kp * 2 * HEAD_DIM, (kp + 1) * 2 * HEAD_DIM)
        vs = slice(KV_W + kp * 2 * HEAD_DIM, KV_W + (kp + 1) * 2 * HEAD_DIM)
        vb = _with_meta(kv_ref[0, pl.ds(start[ri], N_BAND), vs], vm_ref[:, ps], hf)
        q0 = ri * GRID_W + hf * HALF_W
        s = []
        for j in range(WIN_ROWS // 2):
            bias = bias_ref[kp, hf, d0[ri] + 2 * j]
            if j == 0:
                at = META_AT[hf]
                bias = jnp.where((lane >= at) & (lane < at + N_META), 0.0, bias)
            s.append(s_raw[:, j * 128:(j + 1) * 128] + bias)
        m = jnp.max(jnp.maximum(jnp.maximum(s[0], s[1]), jnp.maximum(s[2], s[3])), axis=1, keepdims=True)
        p = jnp.concatenate([jnp.exp(sj - m) for sj in s], axis=1).astype(BF16)
        oa = _dot(p, jnp.concatenate([vb, ones], axis=1))
        o = oa[:, :128] / oa[:, 128:]
        for par in range(2):
            for t in range(Q_PER_KV // 2):
                base = (par * Q_PER_KV + 2 * t) * HALF_W
                a = o[base:base + HALF_W]
                b = o[base + HALF_W:base + 2 * HALF_W]
                if par == 0:
                    tile = jnp.where(lo, a, pltpu.roll(b, HEAD_DIM, 1))
                else:
                    tile = jnp.where(lo, pltpu.roll(a, HEAD_DIM, 1), b)
                c0 = ((2 * kp + par) * 2 + t) * 128
                o_ref[0, q0:q0 + HALF_W, c0:c0 + 128] = tile.astype(BF16)


def _attention(proj3, bias_tab, k_meta, v_meta):
    b, s, _ = proj3.shape
    rows = s // GRID_W
    return pl.pallas_call(
        functools.partial(_attn_kernel, rows=rows),
        grid=(b, rows // ATTN_ROWS),
        in_specs=[
            pl.BlockSpec((1, ATTN_ROWS * GRID_W, ATTN_W), lambda i, r: (i, r, COL_Q // ATTN_W)),
            pl.BlockSpec((1, s, 2 * KV_W), lambda i, r: (i, 0, COL_K // (2 * KV_W))),
            pl.BlockSpec(bias_tab.shape, lambda i, r: (0, 0, 0, 0, 0), pipeline_mode=pl.Buffered(1)),
            pl.BlockSpec((N_META, KV_W), lambda i, r: (0, 0)),
            pl.BlockSpec((N_META, KV_W), lambda i, r: (0, 0)),
        ],
        out_specs=pl.BlockSpec((1, ATTN_ROWS * GRID_W, ATTN_W), lambda i, r: (i, r, 0)),
        out_shape=jax.ShapeDtypeStruct((b, s, ATTN_W), BF16),
        compiler_params=_cparams("parallel", "arbitrary"),
        name="na_attention",
    )(proj3, proj3, bias_tab, k_meta, v_meta)


def _bias_table(rpb):
    c = np.arange(GRID_W)
    cs = np.clip(c - WIN_COLS // 2, 0, GRID_W - WIN_COLS)
    inwin = (c[None, :] >= cs[:, None]) & (c[None, :] < cs[:, None] + WIN_COLS)
    dc = np.clip(c[None, :] - c[:, None] + WIN_COLS - 1, 0, 2 * WIN_COLS - 2)
    pick = jnp.asarray(dc[None] == np.arange(2 * WIN_COLS - 1)[:, None, None], F32)
    slab = jnp.einsum("hdk,kcj->hdcj", rpb.astype(F32), pick, precision=lax.Precision.HIGHEST)
    slab = jnp.where(inwin[None, None], slab, NEG)
    nd = 2 * WIN_ROWS - 1
    slab = slab.reshape(N_KV_HEADS // 2, 2 * Q_PER_KV, nd, 2, HALF_W, GRID_W)
    slab = slab.transpose(0, 3, 2, 1, 4, 5).reshape(N_KV_HEADS // 2, 2, nd, 2 * Q_PER_KV * HALF_W, GRID_W)
    return jnp.concatenate([slab[:, :, :nd - 1], slab[:, :, 1:]], axis=-1)


HALO = 16
CONV_SUB = 128
CONV_K = 256
SHIFT_TAPS = (0, 1, 3)


def _shift_matrix():
    m = np.zeros((len(SHIFT_TAPS) * CONV_SUB, CONV_K), np.float32)
    for k, tap in enumerate(SHIFT_TAPS):
        t = np.arange(CONV_SUB)
        m[k * CONV_SUB + t, t + HALO + tap - D_CONV // 2] = 1.0
    return m


def _conv_kernel(cur_ref, prev_ref, next_ref, lh_ref, rh_ref, w_ref, b_ref, t_ref, o_ref):
    i = pl.program_id(1)
    last = pl.num_programs(1) - 1
    tc, cw = cur_ref.shape[1], cur_ref.shape[2]
    before = jnp.where(i == 0, lh_ref[0], prev_ref[0])
    after = jnp.where(i == last, rh_ref[0], next_ref[0])
    pad = jnp.zeros((CONV_K - CONV_SUB - 2 * HALO, cw), BF16)
    ext = jnp.concatenate([before, cur_ref[0], after, pad], axis=0)
    w = w_ref[...]
    shift = t_ref[...]
    for blk in range(tc // CONV_SUB):
        r0 = blk * CONV_SUB
        sh = _dot(shift, ext[r0:r0 + CONV_K])
        y = b_ref[...] + w[D_CONV // 2:D_CONV // 2 + 1] * cur_ref[0, r0:r0 + CONV_SUB].astype(F32)
        for k, tap in enumerate(SHIFT_TAPS):
            y = y + w[tap:tap + 1] * sh[k * CONV_SUB:(k + 1) * CONV_SUB]
        o_ref[0, r0:r0 + CONV_SUB] = _silu(y).astype(BF16)


def _conv(src, col0, lh, rh, conv_w, conv_b, tc, cw, shared):
    b = lh.shape[0]
    s = src.shape[1]
    hb = tc // HALO
    nhb = s // HALO
    c0 = col0 // cw
    shift = jnp.asarray(_shift_matrix(), BF16)
    bi = (lambda i: 0) if shared else (lambda i: i)
    return pl.pallas_call(
        _conv_kernel,
        grid=(b, s // tc, CONV_DIM // cw),
        in_specs=[
            pl.BlockSpec((1, tc, cw), lambda i, t, c: (bi(i), t, c0 + c)),
            pl.BlockSpec((1, HALO, cw), lambda i, t, c: (bi(i), jnp.maximum(t * hb - 1, 0), c0 + c)),
            pl.BlockSpec((1, HALO, cw), lambda i, t, c: (bi(i), jnp.minimum((t + 1) * hb, nhb - 1), c0 + c)),
            pl.BlockSpec((1, HALO, cw), lambda i, t, c: (i, 0, c)),
            pl.BlockSpec((1, HALO, cw), lambda i, t, c: (i, 0, c)),
            pl.BlockSpec((D_CONV, cw), lambda i, t, c: (0, c)),
            pl.BlockSpec((1, cw), lambda i, t, c: (0, c)),
            pl.BlockSpec(shift.shape, lambda i, t, c: (0, 0)),
        ],
        out_specs=pl.BlockSpec((1, tc, cw), lambda i, t, c: (i, t, c)),
        out_shape=jax.ShapeDtypeStruct((b, s, CONV_DIM), BF16),
        compiler_params=_cparams("parallel", "arbitrary", "arbitrary"),
        name="conv_silu",
    )(src, src, src, lh, rh, conv_w, conv_b, shift)


def _softplus(x):
    return jnp.maximum(x, 0.0) + jnp.log1p(jnp.exp(-jnp.abs(x)))


def _ssd_chunks(x_ref, b_ref, c_ref, dtraw_ref, dtb_ref, alog_ref, tri_ref, exp_ref, state_ref, emit,
                *, reverse, n_valid):
    n_seq, L = x_ref.shape[0], x_ref.shape[1]
    off = SSD_HEADS if reverse else 0
    li = lax.broadcasted_iota(jnp.int32, (L, L), 0)
    si = lax.broadcasted_iota(jnp.int32, (L, L), 1)
    keep = (si >= li) if reverse else (si <= li)
    lane = lax.broadcasted_iota(jnp.int32, (L, 2 * SSD_HEAD_DIM), 1)
    left = lane < SSD_HEAD_DIM
    tri = tri_ref[...]

    def per_head(bi):
        dt = _softplus(dtraw_ref[bi] + dtb_ref[...])
        if n_valid < L:
            dt = jnp.where(lax.broadcasted_iota(jnp.int32, dt.shape, 0) < n_valid, dt, 0.0)
        da = dt * (-jnp.exp(alog_ref[...]))
        acs = sum(_dot(tri, piece) for piece in _split3(da))
        tot = acs[0:1] if reverse else acs[L - 1:L]
        src_t = (acs - jnp.log(dt)).T
        decay_out = jnp.exp(acs).astype(BF16)
        decay_in = (jnp.exp(tot - acs) * dt).astype(BF16)
        return acs, src_t, decay_out, decay_in

    heads = [per_head(bi) for bi in range(n_seq)]
    for g in range(SSD_GROUPS):
        gs = slice(g * GROUP_W, (g + 1) * GROUP_W)
        ns = slice(g * SSD_STATE, (g + 1) * SSD_STATE)
        expand = exp_ref[:, gs]
        for bi in range(n_seq):
            acs, src_t, decay_out, decay_in = heads[bi]
            e_out = _dot(decay_out, expand)
            e_tot = e_out[0:1] if reverse else e_out[L - 1:L]
            xw = (x_ref[bi, :, gs].astype(F32) * _dot(decay_in, expand)).astype(BF16)
            bg = b_ref[bi, :, ns]
            cg = c_ref[bi, :, ns]
            cb = _dot_nt(cg, bg)
            st = state_ref[bi, g]
            y_inter = _dot(cg, st.astype(BF16)) * e_out
            state_ref[bi, g] = st * e_tot + _dot_tn(bg, xw)
            pairs = []
            for j in range(SSD_HEADS_PER_GROUP // 2):
                ms = []
                for h in (off + g * SSD_HEADS_PER_GROUP + 2 * j, off + g * SSD_HEADS_PER_GROUP + 2 * j + 1):
                    seg = acs[:, h:h + 1] - src_t[h:h + 1, :]
                    ms.append((cb * jnp.exp(jnp.where(keep, seg, -jnp.inf))).astype(BF16))
                xp = x_ref[bi, :, g * GROUP_W + j * 128:g * GROUP_W + (j + 1) * 128]
                zero = jnp.zeros_like(xp)
                xbd = jnp.concatenate([jnp.where(left, xp, zero), jnp.where(left, zero, xp)], axis=0)
                pairs.append(_dot(jnp.concatenate(ms, axis=1), xbd))
            emit(bi, gs, jnp.concatenate(pairs, axis=1) + y_inter)


def _ssd_fwd_kernel(x_ref, b_ref, c_ref, dtraw_ref, dtb_ref, alog_ref, tri_ref, exp_ref, init_ref,
                    y_ref, fin_ref, state_ref, *, n_valid):
    c = pl.program_id(1)

    @pl.when(c == 0)
    def _():
        state_ref[...] = init_ref[...]

    def emit(bi, gs, y):
        y_ref[bi, :, gs] = y

    _ssd_chunks(x_ref, b_ref, c_ref, dtraw_ref, dtb_ref, alog_ref, tri_ref, exp_ref, state_ref, emit,
                reverse=False, n_valid=n_valid)

    @pl.when(c == pl.num_programs(1) - 1)
    def _():
        fin_ref[...] = state_ref[...]


def _ssd_bwd_kernel(x_ref, b_ref, c_ref, dtraw_ref, dtb_ref, alog_ref, tri_ref, exp_ref,
                    yf_ref, z_ref, dskip_ref, nw_ref, o_ref, state_ref):
    c = pl.program_id(1)

    @pl.when(c == 0)
    def _():
        state_ref[...] = jnp.zeros_like(state_ref)

    def emit(bi, gs, y):
        y = y + yf_ref[bi, :, gs] + x_ref[bi, :, gs].astype(F32) * dskip_ref[:, gs]
        y = y * z_ref[bi, :, gs].astype(F32)
        ms = jnp.mean(y * y, axis=-1, keepdims=True)
        o_ref[bi, :, gs] = (y * lax.rsqrt(ms + EPS) * nw_ref[:, gs]).astype(BF16)

    _ssd_chunks(x_ref, b_ref, c_ref, dtraw_ref, dtb_ref, alog_ref, tri_ref, exp_ref, state_ref, emit,
                reverse=True, n_valid=x_ref.shape[1])


def _seqs_per_step(b):
    return 2 if b % 2 == 0 else 1


def _ssd_common_specs(ns, L, cmap):
    nb = SSD_GROUPS * SSD_STATE
    return [
        pl.BlockSpec((ns, L, SSD_INNER), lambda i, c: (i, cmap(c), 0)),
        pl.BlockSpec((ns, L, nb), lambda i, c: (i, cmap(c), SSD_INNER // nb)),
        pl.BlockSpec((ns, L, nb), lambda i, c: (i, cmap(c), SSD_INNER // nb + 1)),
    ]


def _const_spec(shape):
    nd = len(shape)
    return pl.BlockSpec(shape, lambda i, c: (0,) * nd)


def _ssd_forward(xc, dt_raw, dt_bias, a_log, tri, expand, init, n_valid=CHUNK):
    b, s, _ = xc.shape
    L = CHUNK
    ns = _seqs_per_step(b)
    st_shape = (SSD_GROUPS, SSD_STATE, GROUP_W)
    nc = s // L
    return pl.pallas_call(
        functools.partial(_ssd_fwd_kernel, n_valid=n_valid),
        grid=(b // ns, nc),
        in_specs=_ssd_common_specs(ns, L, lambda c: c) + [
            pl.BlockSpec((ns, L, 2 * SSD_HEADS), lambda i, c: (i, c, 0)),
            _const_spec((1, 2 * SSD_HEADS)), _const_spec((1, 2 * SSD_HEADS)),
            _const_spec((L, L)), _const_spec((2 * SSD_HEADS, SSD_INNER)),
            pl.BlockSpec((ns,) + st_shape, lambda i, c: (i, 0, 0, 0)),
        ],
        out_specs=[
            pl.BlockSpec((ns, L, SSD_INNER), lambda i, c: (i, c, 0)),
            pl.BlockSpec((ns,) + st_shape, lambda i, c: (i, 0, 0, 0)),
        ],
        out_shape=[
            jax.ShapeDtypeStruct((b, s, SSD_INNER), F32),
            jax.ShapeDtypeStruct((b,) + st_shape, F32),
        ],
        scratch_shapes=[pltpu.VMEM((ns,) + st_shape, F32)],
        compiler_params=_cparams("parallel", "arbitrary"),
        name="ssd_forward",
    )(xc, xc, xc, dt_raw, dt_bias, a_log, tri, expand, init)


def _ssd_backward(xc, dt_raw, dt_bias, a_log, tri, expand, y_f, proj3, d_skip, norm_w):
    b, s, _ = xc.shape
    L = CHUNK
    nc = s // L
    ns = _seqs_per_step(b)
    rev = lambda c: nc - 1 - c
    return pl.pallas_call(
        _ssd_bwd_kernel,
        grid=(b // ns, nc),
        in_specs=_ssd_common_specs(ns, L, rev) + [
            pl.BlockSpec((ns, L, 2 * SSD_HEADS), lambda i, c: (i, rev(c), 0)),
            _const_spec((1, 2 * SSD_HEADS)), _const_spec((1, 2 * SSD_HEADS)),
            _const_spec((L, L)), _const_spec((2 * SSD_HEADS, SSD_INNER)),
            pl.BlockSpec((ns, L, SSD_INNER), lambda i, c: (i, rev(c), 0)),
            pl.BlockSpec((ns, L, SSD_INNER), lambda i, c: (i, rev(c), COL_Z // SSD_INNER)),
            _const_spec((1, SSD_INNER)), _const_spec((1, SSD_INNER)),
        ],
        out_specs=pl.BlockSpec((ns, L, SSD_INNER), lambda i, c: (i, rev(c), 0)),
        out_shape=jax.ShapeDtypeStruct((b, s, SSD_INNER), BF16),
        scratch_shapes=[pltpu.VMEM((ns, SSD_GROUPS, SSD_STATE, GROUP_W), F32)],
        compiler_params=_cparams("parallel", "arbitrary"),
        name="ssd_backward",
    )(xc, xc, xc, dt_raw, dt_bias, a_log, tri, expand, y_f, proj3, d_skip, norm_w)


def _outproj_kernel(a_ref, s_ref, wa_ref, ws_ref, x_ref, o_ref):
    o_ref[...] = x_ref[...] + _dot(a_ref[...], wa_ref[...]) + _dot(s_ref[...], ws_ref[...])


def _out_proj(attn2, ssd2, w_a, w_s, x2d, tm, tn):
    m = x2d.shape[0]
    return pl.pallas_call(
        _outproj_kernel,
        grid=(D_MODEL // tn, m // tm),
        in_specs=[
            pl.BlockSpec((tm, ATTN_W), lambda j, i: (i, 0)),
            pl.BlockSpec((tm, SSD_INNER), lambda j, i: (i, 0)),
            pl.BlockSpec((ATTN_W, tn), lambda j, i: (0, j)),
            pl.BlockSpec((SSD_INNER, tn), lambda j, i: (0, j)),
            pl.BlockSpec((tm, tn), lambda j, i: (i, j)),
        ],
        out_specs=pl.BlockSpec((tm, tn), lambda j, i: (i, j)),
        out_shape=jax.ShapeDtypeStruct((m, D_MODEL), F32),
        compiler_params=_cparams("parallel", "parallel"),
        name="out_proj",
    )(attn2, ssd2, w_a, w_s, x2d)


def _ffn_kernel(h_ref, g_ref, wg_ref, wu_ref, wd_ref, o_ref, f_ref):
    j = pl.program_id(1)

    @pl.when(j == 0)
    def _():
        h = h_ref[...]
        ms = jnp.mean(h * h, axis=-1, keepdims=True)
        f_ref[...] = (h * lax.rsqrt(ms + EPS) * g_ref[...]).astype(BF16)
        o_ref[...] = h

    f = f_ref[...]
    gate = _dot(f, wg_ref[...])
    up = _dot(f, wu_ref[...])
    o_ref[...] += _dot((_silu(gate) * up).astype(BF16), wd_ref[...])


def _ffn(h2d, g_ffn, w_gate, w_up, w_down, tm, tf):
    m = h2d.shape[0]
    return pl.pallas_call(
        _ffn_kernel,
        grid=(m // tm, D_FF // tf),
        in_specs=[
            pl.BlockSpec((tm, D_MODEL), lambda i, j: (i, 0)),
            pl.BlockSpec((1, D_MODEL), lambda i, j: (0, 0)),
            pl.BlockSpec((D_MODEL, tf), lambda i, j: (0, j)),
            pl.BlockSpec((D_MODEL, tf), lambda i, j: (0, j)),
            pl.BlockSpec((tf, D_MODEL), lambda i, j: (j, 0)),
        ],
        out_specs=pl.BlockSpec((tm, D_MODEL), lambda i, j: (i, 0)),
        out_shape=jax.ShapeDtypeStruct((m, D_MODEL), F32),
        scratch_shapes=[pltpu.VMEM((tm, D_MODEL), BF16)],
        compiler_params=_cparams("parallel", "arbitrary"),
        name="ffn",
    )(h2d, g_ffn, w_gate, w_up, w_down)


_EXPAND = np.kron(np.eye(SSD_HEADS), np.ones((1, SSD_HEAD_DIM)))


def _prepare(meta_tokens, g_mix, w_in, q_norm, k_norm, rpb, conv_w, conv_b, dt_bias_f, dt_bias_b,
             a_log_f, a_log_b, d_skip, ssd_norm, w_out, g_ffn, w_gate, w_up, w_down):
    l = 0
    p = dict(
        g_mix=g_mix[l].reshape(1, D_MODEL),
        w_in=w_in[l].astype(BF16),
        q_gain=jnp.tile(q_norm[l].astype(F32) * HEAD_DIM ** -0.5, IN_TN // HEAD_DIM).reshape(1, IN_TN),
        k_gain=jnp.tile(k_norm[l].astype(F32), N_KV_HEADS).reshape(1, KV_W),
        ones_bd=jnp.asarray(np.kron(np.eye(MXU_W // HEAD_DIM), np.full((HEAD_DIM, HEAD_DIM), 1.0 / HEAD_DIM)), BF16),
        bias_tab=_bias_table(rpb[l]),
        conv_w=conv_w[l].astype(F32),
        conv_b=conv_b[l].astype(F32).reshape(1, CONV_DIM),
        dt_bias=jnp.concatenate([dt_bias_f[l], dt_bias_b[l]]).astype(F32).reshape(1, 2 * SSD_HEADS),
        a_log=jnp.concatenate([a_log_f[l], a_log_b[l]]).astype(F32).reshape(1, 2 * SSD_HEADS),
        d_skip=jnp.repeat(d_skip[l].astype(F32), SSD_HEAD_DIM).reshape(1, SSD_INNER),
        ssd_norm=ssd_norm[l].astype(F32).reshape(1, SSD_INNER),
        tril=jnp.asarray(np.tril(np.ones((CHUNK, CHUNK))), BF16),
        triu=jnp.asarray(np.triu(np.ones((CHUNK, CHUNK))), BF16),
        expand_f=jnp.asarray(np.concatenate([_EXPAND, 0 * _EXPAND]), BF16),
        expand_b=jnp.asarray(np.concatenate([0 * _EXPAND, _EXPAND]), BF16),
        w_out_a=w_out[l, :ATTN_W].astype(BF16),
        w_out_s=w_out[l, ATTN_W:].astype(BF16),
        g_ffn=g_ffn[l].reshape(1, D_MODEL),
        w_gate=w_gate[l].astype(BF16),
        w_up=w_up[l].astype(BF16),
        w_down=w_down[l].astype(BF16),
    )
    proj_m, dt_m = _in_proj(meta_tokens.astype(F32), p["g_mix"], p["w_in"], p["q_gain"],
                            p["k_gain"], p["ones_bd"], tm=N_META)
    p["k_meta"] = proj_m[:, COL_K:COL_K + KV_W]
    p["v_meta"] = proj_m[:, COL_V:COL_V + KV_W]
    p["proj_meta"] = proj_m
    p["dt_meta"] = dt_m
    return p


def _trunk(x, p):
    b, s, _ = x.shape
    m = b * s
    x2d = x.reshape(m, D_MODEL)
    proj, dt_raw = _in_proj(x2d, p["g_mix"], p["w_in"], p["q_gain"], p["k_gain"], p["ones_bd"], tm=1024)
    proj3 = proj.reshape(b, s, PROJ_W)
    dt3 = dt_raw.reshape(b, s, 2 * SSD_HEADS)

    attn = _attention(proj3, p["bias_tab"], p["k_meta"], p["v_meta"])

    xbc_meta = p["proj_meta"][:, COL_XBC:COL_XBC + CONV_DIM]
    zeros_h = jnp.zeros((b, HALO, CONV_DIM), BF16)
    xc = _conv(proj3, COL_XBC, jnp.broadcast_to(xbc_meta[None], (b, HALO, CONV_DIM)), zeros_h,
               p["conv_w"], p["conv_b"], tc=min(s, 2048), cw=2048, shared=False)
    meta_in = jnp.concatenate([jnp.broadcast_to(xbc_meta[None], (b, N_META, CONV_DIM)),
                               proj3[:, :CHUNK - N_META, COL_XBC:COL_XBC + CONV_DIM]], axis=1)
    xc_meta = _conv(meta_in, 0, zeros_h, zeros_h, p["conv_w"], p["conv_b"], tc=CHUNK, cw=2048, shared=False)

    dt_meta = jnp.broadcast_to(jnp.pad(p["dt_meta"], ((0, CHUNK - N_META), (0, 0)))[None],
                               (b, CHUNK, 2 * SSD_HEADS))
    zero_state = jnp.zeros((b, SSD_GROUPS, SSD_STATE, GROUP_W), F32)
    _, state0 = _ssd_forward(xc_meta, dt_meta, p["dt_bias"], p["a_log"], p["tril"], p["expand_f"],
                             zero_state, n_valid=N_META)
    y_f, _ = _ssd_forward(xc, dt3, p["dt_bias"], p["a_log"], p["tril"], p["expand_f"], state0)
    ssd = _ssd_backward(xc, dt3, p["dt_bias"], p["a_log"], p["triu"], p["expand_b"], y_f, proj3,
                        p["d_skip"], p["ssd_norm"])

    h1 = _out_proj(attn.reshape(m, ATTN_W), ssd.reshape(m, SSD_INNER), p["w_out_a"], p["w_out_s"], x2d,
                   tm=512, tn=1024)
    out = _ffn(h1, p["g_ffn"], p["w_gate"], p["w_up"], p["w_down"], tm=1024, tf=512)
    return out.reshape(b, s, D_MODEL)


def kernel(x_prompt, x_sample, meta_tokens, g_mix, w_in, q_norm, k_norm, rpb, conv_w, conv_b, dt_bias_f, dt_bias_b, a_log_f, a_log_b, d_skip, ssd_norm, w_out, g_ffn, w_gate, w_up, w_down):
    p = _prepare(meta_tokens, g_mix, w_in, q_norm, k_norm, rpb, conv_w, conv_b, dt_bias_f, dt_bias_b,
                 a_log_f, a_log_b, d_skip, ssd_norm, w_out, g_ffn, w_gate, w_up, w_down)
    return (_trunk(x_prompt, p), _trunk(x_sample, p))
```
